```python
import math
import jax, jax.numpy as jnp
from jax import lax
import numpy as np


D_MODEL = 4096
BATCH = 2
SEQ = 8192
DEPTH = 2

CTX_LEN = 256
GRID_W = 64
N_MIXERS = 2
EXPAND = 2
D_INNER = EXPAND * D_MODEL
S5_GROUP = 16
S5_STATE = 64
S5_GROUPS = D_INNER // S5_GROUP
S5_CHUNK = 128
DT_MIN = 1e-3
DT_MAX = 1e-1
POOL_WINDOWS = (2, 4, 8, 16)
POOL_GROUPS = len(POOL_WINDOWS)
POOL_DIM = D_INNER // POOL_GROUPS
N_S5 = (DEPTH + 1) // 2
N_POOL = DEPTH // 2
RMS_EPS = 1e-6

kernel_name = 'hybrid_s5_pool_prefix_trunk'


def rmsnorm(x, w):
    xf = x.astype(jnp.float32)
    y = xf * lax.rsqrt(jnp.mean(xf * xf, axis=-1, keepdims=True) + RMS_EPS)
    return (y * w.astype(jnp.float32)).astype(x.dtype)


def adaln(cond, w, b):
    m = jax.nn.silu(cond) @ w + b
    return jnp.split(m, 3, axis=-1)


def s5_discretize(lam_re, lam_im, log_step, b_re, b_im, c_re, c_im):
    f32 = lambda a: a.astype(jnp.float32)
    lam = lax.complex(f32(lam_re), f32(lam_im))
    dt = jnp.exp(f32(log_step))[:, None]
    lam_bar = jnp.exp(lam * dt)
    b = lax.complex(f32(b_re), f32(b_im))
    b_bar = ((lam_bar - 1.0) / lam)[..., None] * b
    cm = lax.complex(f32(c_re), f32(c_im))
    return lam_bar, b_bar, cm


def _linear_recurrence(e1, e2):
    a1, b1 = e1
    a2, b2 = e2
    return a1 * a2, a2 * b1 + b2


def s5_scan(u, h0, lam_bar, b_bar, cm, reverse):
    bsz, length, _ = u.shape
    chunk = min(S5_CHUNK, length)
    n_chunks = length // chunk
    uf = u.astype(jnp.float32)
    if reverse:
        uf = uf[:, ::-1]
    ub = uf.reshape(bsz, n_chunks, chunk, S5_GROUPS, S5_GROUP).transpose(1, 0, 2, 3, 4)

    def step(h, u_blk):
        bu = jnp.einsum('gpj,btgj->btgp', b_bar, u_blk.astype(jnp.complex64))
        bu = bu.at[:, 0].add(lam_bar * h)
        a = jnp.broadcast_to(lam_bar, bu.shape)
        _, s = lax.associative_scan(_linear_recurrence, (a, bu), axis=1)
        y = jnp.einsum('gjp,btgp->btgj', cm, s).real
        return s[:, -1], y

    h_final, ys = lax.scan(step, h0, ub)
    y = ys.transpose(1, 0, 2, 3, 4).reshape(bsz, length, D_INNER)
    if reverse:
        y = y[:, ::-1]
    return y, h_final


def s5_bidirectional(u, uc, lam_re, lam_im, log_step, b_re, b_im, c_re, c_im):
    y_lat = 0.0
    y_ctx = 0.0
    for d, rev in enumerate((False, True)):
        lam_bar, b_bar, cm = s5_discretize(lam_re[d], lam_im[d], log_step[d],
                                           b_re[d], b_im[d], c_re[d], c_im[d])
        h0 = jnp.zeros((u.shape[0], S5_GROUPS, S5_STATE), jnp.complex64)
        yc, hc = s5_scan(uc, h0, lam_bar, b_bar, cm, rev)
        yl, _ = s5_scan(u, hc, lam_bar, b_bar, cm, rev)
        y_lat = y_lat + yl
        y_ctx = y_ctx + yc
    return y_lat, y_ctx


def s5_readout(y_scan, u, d_skip, w_glu, b_glu):
    y = y_scan + d_skip.astype(jnp.float32) * u.astype(jnp.float32)
    y = jax.nn.gelu(y).astype(u.dtype)
    return y * jax.nn.sigmoid(y @ w_glu + b_glu)


def window_bounds(n, w):
    pos = jnp.arange(n)
    return jnp.clip(pos - w // 2, 0, n), jnp.clip(pos + w - w // 2, 0, n)


def pool_group_deltas(u, on_grid):
    uf = u.astype(jnp.float32)
    bsz, length, _ = uf.shape
    deltas = []
    if on_grid:
        rows = length // GRID_W
        g = uf.reshape(bsz, rows, GRID_W, D_INNER)
        sat = jnp.pad(jnp.cumsum(jnp.cumsum(g, axis=1), axis=2), ((0, 0), (1, 0), (1, 0), (0, 0)))
        for k, w in enumerate(POOL_WINDOWS):
            s = sat[..., k * POOL_DIM:(k + 1) * POOL_DIM]
            r_lo, r_hi = window_bounds(rows, w)
            c_lo, c_hi = window_bounds(GRID_W, w)
            corner = lambda ri, ci: s[:, ri][:, :, ci]
            total = corner(r_hi, c_hi) - corner(r_lo, c_hi) - corner(r_hi, c_lo) + corner(r_lo, c_lo)
            count = ((r_hi - r_lo)[:, None] * (c_hi - c_lo)[None, :]).astype(jnp.float32)
            mean = (total / count[None, :, :, None]).reshape(bsz, length, POOL_DIM)
            deltas.append(mean - uf[..., k * POOL_DIM:(k + 1) * POOL_DIM])
    else:
        p = jnp.pad(jnp.cumsum(uf, axis=1), ((0, 0), (1, 0), (0, 0)))
        for k, w in enumerate(POOL_WINDOWS):
            sl = slice(k * POOL_DIM, (k + 1) * POOL_DIM)
            lo, hi = window_bounds(length, w)
            mean = (p[:, hi, sl] - p[:, lo, sl]) / (hi - lo).astype(jnp.float32)[None, :, None]
            deltas.append(mean - uf[..., sl])
    return deltas


def pool_mixer(u, on_grid, w_groups, scale):
    deltas = pool_group_deltas(u, on_grid)
    y = jnp.concatenate([d.astype(u.dtype) @ w_groups[k] for k, d in enumerate(deltas)], axis=-1)
    return y * scale


def setup_inputs(seed: int = 0) -> dict:
    key = jax.random.key(seed)
    ks = jax.random.split(key, 24)
    nrm = jax.random.normal
    G, P, J, E, D = S5_GROUPS, S5_STATE, S5_GROUP, D_INNER, D_MODEL
    lam_im_init = jnp.pi * jnp.arange(P, dtype=jnp.float32)
    return {
        'x': nrm(ks[0], (BATCH, SEQ, D), jnp.float32),
        'c': nrm(ks[1], (BATCH, D), jnp.float32),
        'ctx': nrm(ks[2], (BATCH, CTX_LEN, D), jnp.float32),
        'c_ctx': nrm(ks[3], (D,), jnp.float32),
        'norm_w': 1.0 + 0.01 * nrm(ks[4], (DEPTH, D), jnp.float32),
        'w_ada': nrm(ks[5], (DEPTH, D, 3 * D), jnp.float32) * D ** -0.5,
        'b_ada': 0.01 * nrm(ks[6], (DEPTH, 3 * D), jnp.float32),
        'w_in': nrm(ks[7], (DEPTH, D, 2 * E), jnp.float32) * D ** -0.5,
        'w_out': nrm(ks[8], (DEPTH, E, D), jnp.float32) * E ** -0.5,
        's5_lam_re': -0.5 + 0.01 * nrm(ks[9], (N_S5, 2, G, P), jnp.float32),
        's5_lam_im': lam_im_init + 0.01 * nrm(ks[10], (N_S5, 2, G, P), jnp.float32),
        's5_log_step': jax.random.uniform(ks[11], (N_S5, 2, G), jnp.float32,
                                          minval=math.log(DT_MIN), maxval=math.log(DT_MAX)),
        's5_b_re': nrm(ks[12], (N_S5, 2, G, P, J), jnp.float32) * (2 * J) ** -0.5,
        's5_b_im': nrm(ks[13], (N_S5, 2, G, P, J), jnp.float32) * (2 * J) ** -0.5,
        's5_c_re': nrm(ks[14], (N_S5, 2, G, J, P), jnp.float32) * P ** -0.5,
        's5_c_im': nrm(ks[15], (N_S5, 2, G, J, P), jnp.float32) * P ** -0.5,
        's5_d': nrm(ks[16], (N_S5, E), jnp.float32),
        's5_w_glu': nrm(ks[17], (N_S5, E, E), jnp.float32) * E ** -0.5,
        's5_b_glu': 0.01 * nrm(ks[18], (N_S5, E), jnp.float32),
        'pool_w': nrm(ks[19], (N_POOL, POOL_GROUPS, POOL_DIM, POOL_DIM), jnp.float32) * POOL_DIM ** -0.5,
        'pool_scale': 1.0 + 0.02 * nrm(ks[20], (N_POOL, E), jnp.float32),
        'final_norm_w': 1.0 + 0.01 * nrm(ks[21], (D,), jnp.float32),
    }


def reference(x, c, ctx, c_ctx, norm_w, w_ada, b_ada, w_in, w_out,
              s5_lam_re, s5_lam_im, s5_log_step, s5_b_re, s5_b_im, s5_c_re, s5_c_im,
              s5_d, s5_w_glu, s5_b_glu, pool_w, pool_scale, final_norm_w):
    for i in range(DEPTH):
        kind = i % N_MIXERS
        j = i // N_MIXERS
        ctx_later = any(l % N_MIXERS == 0 for l in range(i + 1, DEPTH))
        shift, scale, gate = adaln(c, w_ada[i], b_ada[i])
        h = rmsnorm(x, norm_w[i]) * (1.0 + scale[:, None]) + shift[:, None]
        u, z = jnp.split(h @ w_in[i], 2, axis=-1)
        if kind == 0 or ctx_later:
            shift_c, scale_c, gate_c = adaln(c_ctx, w_ada[i], b_ada[i])
            hc = rmsnorm(ctx, norm_w[i]) * (1.0 + scale_c) + shift_c
            uc, zc = jnp.split(hc @ w_in[i], 2, axis=-1)
        if kind == 0:
            y_lat_scan, y_ctx_scan = s5_bidirectional(
                u, uc, s5_lam_re[j], s5_lam_im[j], s5_log_step[j],
                s5_b_re[j], s5_b_im[j], s5_c_re[j], s5_c_im[j])
            y_lat = s5_readout(y_lat_scan, u, s5_d[j], s5_w_glu[j], s5_b_glu[j])
            if ctx_later:
                y_ctx = s5_readout(y_ctx_scan, uc, s5_d[j], s5_w_glu[j], s5_b_glu[j])
        else:
            y_lat = pool_mixer(u, True, pool_w[j], pool_scale[j])
            if ctx_later:
                y_ctx = pool_mixer(uc, False, pool_w[j], pool_scale[j])
        x = x + gate[:, None] * ((y_lat * jax.nn.silu(z)) @ w_out[i])
        if ctx_later:
            ctx = ctx + gate_c * ((y_ctx * jax.nn.silu(zc)) @ w_out[i])
    return rmsnorm(x, final_norm_w)
```

```python
import functools

import jax
import jax.numpy as jnp
import numpy as np
from jax import lax
from jax.experimental import pallas as pl
from jax.experimental.pallas import tpu as pltpu

F32 = jnp.float32
BF16 = jnp.bfloat16

GRID_W = 64
POOL_WINDOWS = (2, 4, 8, 16)
RMS_EPS = 1e-6
S5_T = 16
LANES = 128
SUBLANES = 8
VMEM_LIMIT = 56 * 1024 * 1024


def _params(sem):
    return pltpu.CompilerParams(dimension_semantics=sem, vmem_limit_bytes=VMEM_LIMIT)


def _tile(n, pref):
    t = min(n, pref)
    while n % t:
        t //= 2
    return t


def _ada_kernel(c_ref, w_ref, b_ref, o_ref):
    s = jax.nn.silu(c_ref[...]).astype(BF16)
    o_ref[0] = jnp.dot(s, w_ref[0].astype(BF16), preferred_element_type=F32) + b_ref[0]


def _ada_call(cond, w_ada, b_ada):
    depth, d, n = w_ada.shape
    tn = _tile(n, 512)
    return pl.pallas_call(
        _ada_kernel,
        grid=(depth, n // tn),
        in_specs=[
            pl.BlockSpec((SUBLANES, d), lambda i, j: (0, 0)),
            pl.BlockSpec((1, d, tn), lambda i, j: (i, 0, j)),
            pl.BlockSpec((1, 1, tn), lambda i, j: (i, 0, j)),
        ],
        out_specs=pl.BlockSpec((1, SUBLANES, tn), lambda i, j: (i, 0, j)),
        out_shape=jax.ShapeDtypeStruct((depth, SUBLANES, n), F32),
        compiler_params=_params(("parallel", "parallel")),
        name="adaln",
    )(cond, w_ada, b_ada.reshape(depth, 1, n))


def _norm_kernel(*refs, modulate):
    if modulate:
        x_ref, w_ref, scale_ref, shift_ref, o_ref = refs
    else:
        x_ref, w_ref, o_ref = refs
    x = x_ref[0]
    y = x * lax.rsqrt(jnp.mean(x * x, axis=-1, keepdims=True) + RMS_EPS) * w_ref[...]
    if modulate:
        y = y * (1.0 + scale_ref[0]) + shift_ref[0]
    o_ref[0] = y.astype(o_ref.dtype)


def _norm_call(x, w, scale=None, shift=None, *, out_dtype):
    b, s, d = x.shape
    tm = _tile(s, 256)
    modulate = scale is not None
    row = pl.BlockSpec((1, tm, d), lambda i, j: (i, j, 0))
    in_specs = [row, pl.BlockSpec((1, d), lambda i, j: (0, 0))]
    args = [x, w.reshape(1, d)]
    if modulate:
        per_batch = pl.BlockSpec((1, 1, d), lambda i, j: (i, 0, 0))
        in_specs += [per_batch, per_batch]
        args += [scale.reshape(b, 1, d), shift.reshape(b, 1, d)]
    return pl.pallas_call(
        functools.partial(_norm_kernel, modulate=modulate),
        grid=(b, s // tm),
        in_specs=in_specs,
        out_specs=row,
        out_shape=jax.ShapeDtypeStruct((b, s, d), out_dtype),
        compiler_params=_params(("parallel", "parallel")),
        name="rmsnorm_mod" if modulate else "rmsnorm",
    )(*args)


def _mm_kernel(*refs, nk, n_extra, epilogue):
    a_ref, w_ref = refs[0], refs[1]
    extras = refs[2:2 + n_extra]
    o_ref = refs[2 + n_extra]
    part = jnp.dot(a_ref[...], w_ref[...], preferred_element_type=F32)

    def finish(acc):
        o_ref[...] = epilogue(acc, *[e[...] for e in extras]).astype(o_ref.dtype)

    if nk == 1:
        finish(part)
        return
    acc_ref = refs[3 + n_extra]
    k = pl.program_id(2)

    @pl.when(k == 0)
    def _():
        acc_ref[...] = part

    @pl.when(jnp.logical_and(k > 0, k < nk - 1))
    def _():
        acc_ref[...] += part

    @pl.when(k == nk - 1)
    def _():
        finish(acc_ref[...] + part)


def _mm_call(a, w, extras, extra_specs, *, grid, a_spec, w_spec, out_spec, out_shape, out_dtype,
             epilogue, name):
    nk = grid[2]
    tm, tn = out_spec.block_shape
    scratch = [pltpu.VMEM((tm, tn), F32)] if nk > 1 else []
    return pl.pallas_call(
        functools.partial(_mm_kernel, nk=nk, n_extra=len(extras), epilogue=epilogue),
        grid=grid,
        in_specs=[a_spec, w_spec, *extra_specs],
        out_specs=out_spec,
        out_shape=jax.ShapeDtypeStruct(out_shape, out_dtype),
        scratch_shapes=scratch,
        compiler_params=_params(("parallel", "parallel", "arbitrary")),
        name=name,
    )(a, w, *extras)


def _in_proj(h, w):
    m, d = h.shape
    n = w.shape[1]
    tm, tn = _tile(m, 1024), _tile(n, 1024)
    return _mm_call(
        h, w, [], [],
        grid=(m // tm, n // tn, 1),
        a_spec=pl.BlockSpec((tm, d), lambda i, j, k: (i, 0)),
        w_spec=pl.BlockSpec((d, tn), lambda i, j, k: (0, j)),
        out_spec=pl.BlockSpec((tm, tn), lambda i, j, k: (i, j)),
        out_shape=(m, n), out_dtype=BF16, epilogue=lambda acc: acc, name="in_proj")


def _glu_epilogue(acc, bias, y, z):
    y = y.astype(F32)
    return y * jax.nn.sigmoid(acc + bias) * jax.nn.silu(z.astype(F32))


def _glu_gate(y, uz, w, bias):
    m, e = y.shape
    tm, tn, tk = _tile(m, 1024), _tile(e, 1024), _tile(e, 2048)
    zoff = e // tn
    return _mm_call(
        y, w, [bias.reshape(1, e), y, uz],
        [pl.BlockSpec((1, tn), lambda i, j, k: (0, j)),
         pl.BlockSpec((tm, tn), lambda i, j, k: (i, j)),
         pl.BlockSpec((tm, tn), lambda i, j, k: (i, j + zoff))],
        grid=(m // tm, e // tn, e // tk),
        a_spec=pl.BlockSpec((tm, tk), lambda i, j, k: (i, k)),
        w_spec=pl.BlockSpec((tk, tn), lambda i, j, k: (k, j)),
        out_spec=pl.BlockSpec((tm, tn), lambda i, j, k: (i, j)),
        out_shape=(m, e), out_dtype=BF16, epilogue=_glu_epilogue, name="s5_glu")


def _res_epilogue(acc, x, gate):
    return x + gate[0] * acc


def _out_proj(v, w, x, gate, seq):
    m, e = v.shape
    d = w.shape[1]
    tm, tn, tk = _tile(seq, 1024), _tile(d, 1024), _tile(e, 2048)
    per_b = seq // tm
    return _mm_call(
        v, w, [x, gate.reshape(-1, 1, d)],
        [pl.BlockSpec((tm, tn), lambda i, j, k: (i, j)),
         pl.BlockSpec((1, 1, tn), lambda i, j, k: (i // per_b, 0, j))],
        grid=(m // tm, d // tn, e // tk),
        a_spec=pl.BlockSpec((tm, tk), lambda i, j, k: (i, k)),
        w_spec=pl.BlockSpec((tk, tn), lambda i, j, k: (k, j)),
        out_spec=pl.BlockSpec((tm, tn), lambda i, j, k: (i, j)),
        out_shape=(m, d), out_dtype=F32, epilogue=_res_epilogue, name="out_proj")


def _pool_epilogue(acc, scale, z):
    return acc * scale * jax.nn.silu(z.astype(F32))


def _pool_proj(delta, uz, w, scale):
    m, e = delta.shape
    ng, dg, _ = w.shape
    tm, tn = _tile(m, 1024), _tile(dg, 1024)
    per_g = dg // tn
    zoff = e // tn
    return _mm_call(
        delta, w.reshape(ng * dg, dg), [scale.reshape(1, e), uz],
        [pl.BlockSpec((1, tn), lambda i, j, k: (0, j)),
         pl.BlockSpec((tm, tn), lambda i, j, k: (i, j + zoff))],
        grid=(m // tm, e // tn, 1),
        a_spec=pl.BlockSpec((tm, dg), lambda i, j, k: (i, j // per_g)),
        w_spec=pl.BlockSpec((dg, tn), lambda i, j, k: (j // per_g, j % per_g)),
        out_spec=pl.BlockSpec((tm, tn), lambda i, j, k: (i, j)),
        out_shape=(m, e), out_dtype=BF16, epilogue=_pool_epilogue, name="pool_proj")


def _s5_pitch(rows):
    pitch = -(-rows // SUBLANES) * SUBLANES
    if (pitch // SUBLANES) % 2 == 0:
        pitch += SUBLANES
    return pitch


def _s5_kernel(u_ref, wm_ref, ws_ref, wo_ref, coef_ref, y_ref, s_ref, *, nb, n_ctx, n_lat):
    gb = u_ref.shape[0]
    n_seq = n_ctx + n_lat
    pitch = s_ref.shape[1] // gb

    for g in range(gb):
        s = jnp.dot(u_ref[g], ws_ref[g], preferred_element_type=F32)
        for q in range(4):
            s_ref[q, g * pitch:g * pitch + nb * n_seq, :] = s[:, q * LANES:(q + 1) * LANES]

    ar_f, ai_f, ar_r, ai_r = coef_ref[0], coef_ref[1], coef_ref[2], coef_ref[3]

    def rows(slab, row):
        return s_ref.at[slab][pl.ds(row, gb, stride=pitch), :]

    def step(i, carry):
        rf = i
        rr = jnp.where(i < n_ctx, n_ctx - 1 - i, n_seq + n_ctx - 1 - i)
        out = []
        for b in range(nb):
            hf, hfs, hr, hrs = carry[4 * b:4 * b + 4]
            row_f = b * n_seq + rf
            row_r = b * n_seq + rr
            sf, sfs = rows(0, row_f), rows(2, row_f)
            sr, srs = rows(1, row_r), rows(3, row_r)
            s_ref.at[0][pl.ds(row_f, gb, stride=pitch), :] = hf
            s_ref.at[1][pl.ds(row_r, gb, stride=pitch), :] = hr
            out += [ar_f * hf + ai_f * hfs + sf, ar_f * hfs - ai_f * hf + sfs,
                    ar_r * hr + ai_r * hrs + sr, ar_r * hrs - ai_r * hr + srs]
        return tuple(out)

    zero = jnp.zeros((gb, LANES), F32)
    lax.fori_loop(0, n_seq, step, (zero,) * (4 * nb))

    for g in range(gb):
        for b in range(nb):
            lo = b * n_seq + n_ctx
            u = u_ref[g, lo:lo + n_lat, :]
            h = jnp.concatenate([s_ref[0, g * pitch + lo:g * pitch + lo + n_lat, :],
                                 s_ref[1, g * pitch + lo:g * pitch + lo + n_lat, :]], axis=1)
            acc = jnp.dot(u, wm_ref[g], preferred_element_type=F32)
            acc += jnp.dot(h.astype(BF16), wo_ref[g], preferred_element_type=F32)
            y_ref[g, b * n_lat:(b + 1) * n_lat, :] = jax.nn.gelu(acc).astype(y_ref.dtype)


def _s5_call(ug, wm, ws, wo, coef, *, nb, n_ctx, n_lat):
    g, rows, width = ug.shape
    gb = SUBLANES
    assert g % gb == 0 and rows == nb * (n_ctx + n_lat) and width == 2 * LANES
    pitch = _s5_pitch(rows)
    grp = lambda shape: pl.BlockSpec((gb,) + shape, lambda i: (i, 0, 0))
    return pl.pallas_call(
        functools.partial(_s5_kernel, nb=nb, n_ctx=n_ctx, n_lat=n_lat),
        grid=(g // gb,),
        in_specs=[grp((rows, width)), grp((width, width)), grp((width, 4 * LANES)), grp((width, width)),
                  pl.BlockSpec((4, gb, LANES), lambda i: (0, i, 0))],
        out_specs=grp((nb * n_lat, width)),
        out_shape=jax.ShapeDtypeStruct((g, nb * n_lat, width), BF16),
        scratch_shapes=[pltpu.VMEM((4, gb * pitch, LANES), F32)],
        compiler_params=_params(("parallel",)),
        name="s5_scan",
    )(ug, wm, ws, wo, coef)


def _s5_weights(lam_re, lam_im, log_step, b_re, b_im, c_re, c_im, d_skip):
    t = S5_T
    _, g, p, j = b_re.shape
    hi = lax.Precision.HIGHEST
    lam = lax.complex(lam_re.astype(F32), lam_im.astype(F32))
    dt = jnp.exp(log_step.astype(F32))[..., None]
    lam_bar = jnp.exp(lam * dt)
    bb = ((lam_bar - 1.0) / lam)[..., None] * lax.complex(b_re.astype(F32), b_im.astype(F32))
    cm = lax.complex(c_re.astype(F32), c_im.astype(F32))
    steps = jnp.arange(t + 1, dtype=F32)[:, None, None, None]
    pw = jnp.exp(lam * dt * steps)

    z = cm[None] * pw[:t, :, :, None, :]
    kern = (jnp.einsum('tdgjp,dgpi->dtgji', z.real, bb.real, precision=hi)
            - jnp.einsum('tdgjp,dgpi->dtgji', z.imag, bb.imag, precision=hi))
    tau = jnp.arange(t)[None, :] - jnp.arange(t)[:, None]
    fwd = jnp.where((tau >= 0)[:, :, None, None, None], kern[0][jnp.clip(tau, 0, t - 1)], 0.0)
    rev = jnp.where((tau <= 0)[:, :, None, None, None], kern[1][jnp.clip(-tau, 0, t - 1)], 0.0)
    skip = (jnp.eye(t, dtype=F32)[:, :, None, None, None] * jnp.eye(j, dtype=F32)[None, None, None]
            * d_skip.astype(F32).reshape(g, j)[None, None, :, :, None])
    wm = (fwd + rev + skip).transpose(2, 0, 4, 1, 3).reshape(g, t * j, t * j)

    xf = (pw[t - 1::-1, 0][:t, :, :, None] * bb[0][None]).transpose(1, 0, 3, 2)
    xr = (pw[:t, 1][:, :, :, None] * bb[1][None]).transpose(1, 0, 3, 2)
    ws = jnp.concatenate([xf.real, xf.imag, xr.real, xr.imag, xf.imag, xf.real, xr.imag, xr.real],
                         axis=-1).reshape(g, t * j, 8 * p)

    zf = (cm[0][None] * pw[1:, 0][:, :, None, :]).transpose(1, 3, 0, 2)
    zr = (cm[1][None] * pw[t:0:-1, 1][:, :, None, :]).transpose(1, 3, 0, 2)
    wo = jnp.concatenate([zf.real, -zf.imag, zr.real, -zr.imag], axis=1).reshape(g, 4 * p, t * j)

    a = pw[t]
    coef = jnp.stack([jnp.concatenate([a[0].real, a[0].real], -1), jnp.concatenate([-a[0].imag, a[0].imag], -1),
                      jnp.concatenate([a[1].real, a[1].real], -1), jnp.concatenate([-a[1].imag, a[1].imag], -1)])
    return wm.astype(BF16), ws.astype(BF16), wo.astype(BF16), coef


def _pool_kernel(u_ref, pc_ref, invc_ref, o_ref, cs_ref, *, n_rows, per_group):
    slab = pc_ref.shape[1]
    group = pl.program_id(1) // per_group

    def run(w):
        for s in range(n_rows * GRID_W // slab):
            sl = slice(s * slab, (s + 1) * slab)
            cs_ref[sl, :] = jnp.dot(pc_ref[0], u_ref[0, sl, :], preferred_element_type=F32)
        lo, hi = w // 2, w - w // 2
        inv_c = invc_ref[0]

        def grid_row(r):
            return pl.ds(pl.multiple_of(r * GRID_W, GRID_W), GRID_W)

        def emit(r, taps):
            acc = cs_ref[grid_row(r + taps[0]), :]
            for dr in taps[1:]:
                acc += cs_ref[grid_row(r + dr), :]
            own = grid_row(r)
            mean = acc * (inv_c * (1.0 / len(taps)))
            o_ref[0, own, :] = (mean - u_ref[0, own, :].astype(F32)).astype(o_ref.dtype)

        full = list(range(-lo, hi))
        first_full, last_full = lo, n_rows - hi
        for r in list(range(first_full)) + list(range(last_full + 1, n_rows)):
            emit(r, [dr for dr in full if 0 <= r + dr < n_rows])

        def body(r, carry):
            emit(r, full)
            return carry

        lax.fori_loop(first_full, last_full + 1, body, 0)

    for k, w in enumerate(POOL_WINDOWS):
        pl.when(group == k)(functools.partial(run, w))


def _pool_consts(slab):
    pcs, invs = [], []
    pos = np.arange(GRID_W)
    for w in POOL_WINDOWS:
        lo, hi = np.clip(pos - w // 2, 0, GRID_W), np.clip(pos + w - w // 2, 0, GRID_W)
        inside = (pos[None, :] >= lo[:, None]) & (pos[None, :] < hi[:, None])
        pcs.append(np.kron(np.eye(slab // GRID_W), inside.astype(np.float32)))
        invs.append((1.0 / (hi - lo)).astype(np.float32).reshape(GRID_W, 1))
    return jnp.asarray(np.stack(pcs), BF16), jnp.asarray(np.stack(invs), F32)


def _pool_call(uz, e):
    b, l, _ = uz.shape
    n_rows = l // GRID_W
    dg = e // len(POOL_WINDOWS)
    tc = _tile(dg, 256)
    slab = _tile(l, 256)
    pc, invc = _pool_consts(slab)
    per_group = dg // tc
    blk = pl.BlockSpec((1, l, tc), lambda i, j: (i, 0, j))
    return pl.pallas_call(
        functools.partial(_pool_kernel, n_rows=n_rows, per_group=per_group),
        grid=(b, e // tc),
        in_specs=[blk,
                  pl.BlockSpec((1, slab, slab), lambda i, j: (j // per_group, 0, 0)),
                  pl.BlockSpec((1, GRID_W, 1), lambda i, j: (j // per_group, 0, 0))],
        out_specs=blk,
        out_shape=jax.ShapeDtypeStruct((b, l, e), BF16),
        scratch_shapes=[pltpu.VMEM((l, tc), F32)],
        compiler_params=_params(("parallel", "parallel")),
        name="pool_delta",
    )(uz, pc, invc)


def kernel(x, c, ctx, c_ctx, norm_w, w_ada, b_ada, w_in, w_out, s5_lam_re, s5_lam_im, s5_log_step,
           s5_b_re, s5_b_im, s5_c_re, s5_c_im, s5_d, s5_w_glu, s5_b_glu, pool_w, pool_scale,
           final_norm_w):
    bsz, seq, d = x.shape
    n_ctx_tok = ctx.shape[1]
    e = w_in.shape[2] // 2
    n_groups, jch = s5_b_re.shape[2], s5_b_re.shape[4]
    assert bsz + 1 <= SUBLANES and S5_T * jch == 2 * LANES and 2 * s5_b_re.shape[3] == LANES
    assert seq % S5_T == 0 and n_ctx_tok % S5_T == 0 and seq % GRID_W == 0

    cond = jnp.zeros((SUBLANES, d), F32).at[:bsz].set(c).at[bsz].set(c_ctx)
    mods = _ada_call(cond, w_ada, b_ada)
    shift, scale, gate = mods[..., :d], mods[..., d:2 * d], mods[..., 2 * d:]

    w_in0 = w_in[0].astype(BF16)
    h = _norm_call(x, norm_w[0], scale[0, :bsz], shift[0, :bsz], out_dtype=BF16)
    uz = _in_proj(h.reshape(bsz * seq, d), w_in0)
    ctx_mod = lambda v: jnp.broadcast_to(v[0, bsz], (bsz, d))
    hc = _norm_call(ctx, norm_w[0], ctx_mod(scale), ctx_mod(shift), out_dtype=BF16)
    uc = _in_proj(hc.reshape(bsz * n_ctx_tok, d), w_in0[:, :e])

    n_ctx, n_lat = n_ctx_tok // S5_T, seq // S5_T
    u_all = jnp.concatenate([uc.reshape(bsz, n_ctx_tok, e), uz.reshape(bsz, seq, 2 * e)[..., :e]], axis=1)
    ug = u_all.reshape(bsz, n_ctx + n_lat, S5_T, n_groups, jch).transpose(3, 0, 1, 2, 4)
    ug = ug.reshape(n_groups, bsz * (n_ctx + n_lat), S5_T * jch)
    wm, ws, wo, coef = _s5_weights(s5_lam_re[0], s5_lam_im[0], s5_log_step[0], s5_b_re[0], s5_b_im[0],
                                   s5_c_re[0], s5_c_im[0], s5_d[0])
    yg = _s5_call(ug, wm, ws, wo, coef, nb=bsz, n_ctx=n_ctx, n_lat=n_lat)
    y = yg.reshape(n_groups, bsz, n_lat, S5_T, jch).transpose(1, 2, 3, 0, 4).reshape(bsz * seq, e)

    v = _glu_gate(y, uz, s5_w_glu[0].astype(BF16), s5_b_glu[0])
    x1 = _out_proj(v, w_out[0].astype(BF16), x.reshape(bsz * seq, d), gate[0, :bsz], seq)

    h = _norm_call(x1.reshape(bsz, seq, d), norm_w[1], scale[1, :bsz], shift[1, :bsz], out_dtype=BF16)
    uz = _in_proj(h.reshape(bsz * seq, d), w_in[1].astype(BF16))
    delta = _pool_call(uz.reshape(bsz, seq, 2 * e), e)
    v = _pool_proj(delta.reshape(bsz * seq, e), uz, pool_w[0].astype(BF16), pool_scale[0])
    x2 = _out_proj(v, w_out[1].astype(BF16), x1, gate[1, :bsz], seq)

    return _norm_call(x2.reshape(bsz, seq, d), final_norm_w, out_dtype=x.dtype)
```

```python
import functools

import jax
import jax.numpy as jnp
import numpy as np
from jax import lax
from jax.experimental import pallas as pl
from jax.experimental.pallas import tpu as pltpu

F32 = jnp.float32
BF16 = jnp.bfloat16

GRID_W = 64
POOL_WINDOWS = (2, 4, 8, 16)
RMS_EPS = 1e-6
S5_T = 16
LANES = 128
SUBLANES = 8
VMEM_LIMIT = 56 * 1024 * 1024


def _params(sem):
    return pltpu.CompilerParams(dimension_semantics=sem, vmem_limit_bytes=VMEM_LIMIT)


def _tile(n, pref):
    t = min(n, pref)
    while n % t:
        t //= 2
    return t


def _ada_kernel(c_ref, w_ref, b_ref, o_ref):
    s = jax.nn.silu(c_ref[...]).astype(BF16)
    o_ref[0] = jnp.dot(s, w_ref[0].astype(BF16), preferred_element_type=F32) + b_ref[0]


def _ada_call(cond, w_ada, b_ada):
    depth, d, n = w_ada.shape
    tn = _tile(n, 512)
    return pl.pallas_call(
        _ada_kernel,
        grid=(depth, n // tn),
        in_specs=[
            pl.BlockSpec((SUBLANES, d), lambda i, j: (0, 0)),
            pl.BlockSpec((1, d, tn), lambda i, j: (i, 0, j)),
            pl.BlockSpec((1, 1, tn), lambda i, j: (i, 0, j)),
        ],
        out_specs=pl.BlockSpec((1, SUBLANES, tn), lambda i, j: (i, 0, j)),
        out_shape=jax.ShapeDtypeStruct((depth, SUBLANES, n), F32),
        compiler_params=_params(("parallel", "parallel")),
        name="adaln",
    )(cond, w_ada, b_ada.reshape(depth, 1, n))


def _norm_kernel(*refs, modulate):
    x_ref, w_ref = refs[0], refs[1]
    o_ref = refs[-1]
    x = x_ref[0]
    y = x * lax.rsqrt(jnp.mean(x * x, axis=-1, keepdims=True) + RMS_EPS) * w_ref[...]
    if modulate:
        scale_ref, shift_ref = refs[2], refs[3]
        y = y * (1.0 + scale_ref[0]) + shift_ref[0]
    o_ref[0] = y.astype(o_ref.dtype)


def _norm_call(x, w, scale=None, shift=None, *, out_dtype, out_rows=None, row_offset=0, into=None):
    b, s, d = x.shape
    out_rows = s if out_rows is None else out_rows
    tm = _tile(s, 256)
    assert row_offset % tm == 0
    off = row_offset // tm
    modulate = scale is not None
    in_specs = [pl.BlockSpec((1, tm, d), lambda i, j: (i, j, 0)), pl.BlockSpec((1, d), lambda i, j: (0, 0))]
    args = [x, w.reshape(1, d)]
    if modulate:
        per_batch = pl.BlockSpec((1, 1, d), lambda i, j: (i, 0, 0))
        in_specs += [per_batch, per_batch]
        args += [scale.reshape(b, 1, d), shift.reshape(b, 1, d)]
    aliases = {}
    if into is not None:
        aliases = {len(args): 0}
        in_specs.append(pl.BlockSpec(memory_space=pl.ANY))
        args.append(into)
    return pl.pallas_call(
        functools.partial(_norm_kernel, modulate=modulate),
        grid=(b, s // tm),
        in_specs=in_specs,
        out_specs=pl.BlockSpec((1, tm, d), lambda i, j: (i, j + off, 0)),
        out_shape=jax.ShapeDtypeStruct((b, out_rows, d), out_dtype),
        input_output_aliases=aliases,
        compiler_params=_params(("parallel", "parallel")),
        name="rmsnorm_mod" if modulate else "rmsnorm",
    )(*args)


def _mm_kernel(*refs, nk, n_extra, epilogue):
    a_ref, w_ref = refs[0], refs[1]
    extras = refs[2:2 + n_extra]
    o_ref = refs[2 + n_extra]
    a = a_ref[...]
    part = jnp.dot(a.reshape(-1, a.shape[-1]), w_ref[...], preferred_element_type=F32)

    def finish(acc):
        out = epilogue(acc, *[e[...] for e in extras])
        o_ref[...] = out.reshape(o_ref.shape).astype(o_ref.dtype)

    if nk == 1:
        finish(part)
        return
    acc_ref = refs[3 + n_extra]
    k = pl.program_id(2)

    @pl.when(k == 0)
    def _():
        acc_ref[...] = part

    @pl.when(jnp.logical_and(k > 0, k < nk - 1))
    def _():
        acc_ref[...] += part

    @pl.when(k == nk - 1)
    def _():
        finish(acc_ref[...] + part)


def _mm_call(a, w, extras, extra_specs, *, grid, a_spec, w_spec, out_spec, out_shape, out_dtype,
             epilogue, name):
    nk = grid[2]
    blk = [n for n in out_spec.block_shape if n is not None]
    scratch = [pltpu.VMEM((int(np.prod(blk[:-1])), blk[-1]), F32)] if nk > 1 else []
    return pl.pallas_call(
        functools.partial(_mm_kernel, nk=nk, n_extra=len(extras), epilogue=epilogue),
        grid=grid,
        in_specs=[a_spec, w_spec, *extra_specs],
        out_specs=out_spec,
        out_shape=jax.ShapeDtypeStruct(out_shape, out_dtype),
        scratch_shapes=scratch,
        compiler_params=_params(("parallel", "parallel", "arbitrary")),
        name=name,
    )(a, w, *extras)


def _in_proj(h, w):
    m, d = h.shape
    n = w.shape[1]
    tm, tn = _tile(m, 1024), _tile(n, 1024)
    return _mm_call(
        h, w, [], [],
        grid=(m // tm, n // tn, 1),
        a_spec=pl.BlockSpec((tm, d), lambda i, j, k: (i, 0)),
        w_spec=pl.BlockSpec((d, tn), lambda i, j, k: (0, j)),
        out_spec=pl.BlockSpec((tm, tn), lambda i, j, k: (i, j)),
        out_shape=(m, n), out_dtype=BF16, epilogue=lambda acc: acc, name="in_proj")


def _in_proj_tmajor(h, w):
    b, s, d = h.shape
    n = w.shape[1]
    c = s // S5_T
    tn = _tile(n, 1024)
    return _mm_call(
        h.reshape(b, c, S5_T * d), w, [], [],
        grid=(S5_T, n // tn, 1),
        a_spec=pl.BlockSpec((b, c, d), lambda t, j, k: (0, 0, t)),
        w_spec=pl.BlockSpec((d, tn), lambda t, j, k: (0, j)),
        out_spec=pl.BlockSpec((None, b, c, tn), lambda t, j, k: (t, 0, 0, j)),
        out_shape=(S5_T, b, c, n), out_dtype=BF16, epilogue=lambda acc: acc, name="in_proj_tmajor")


def _glu_epilogue(acc, bias, y, z):
    y = y.astype(F32)
    z = z.astype(F32).reshape(acc.shape)
    return y * jax.nn.sigmoid(acc + bias) * jax.nn.silu(z)


def _glu_gate(y, uz, w, bias, n_lat):
    m, e = y.shape
    t, b = uz.shape[0], uz.shape[1]
    tm = b * n_lat
    assert m == t * tm
    tn, tk = _tile(e, 1024), _tile(e, 2048)
    zoff = e // tn
    return _mm_call(
        y, w, [bias.reshape(1, e), y, uz],
        [pl.BlockSpec((1, tn), lambda i, j, k: (0, j)),
         pl.BlockSpec((tm, tn), lambda i, j, k: (i, j)),
         pl.BlockSpec((None, b, n_lat, tn), lambda i, j, k: (i, 0, 0, j + zoff))],
        grid=(t, e // tn, e // tk),
        a_spec=pl.BlockSpec((tm, tk), lambda i, j, k: (i, k)),
        w_spec=pl.BlockSpec((tk, tn), lambda i, j, k: (k, j)),
        out_spec=pl.BlockSpec((tm, tn), lambda i, j, k: (i, j)),
        out_shape=(m, e), out_dtype=BF16, epilogue=_glu_epilogue, name="s5_glu")


def _res_epilogue(acc, x, gate):
    return x + gate * acc.reshape(x.shape)


def _out_proj(v, w, x, gate):
    b, s, d = x.shape
    e = v.shape[1]
    tm, tn, tk = _tile(s, 1024), _tile(d, 1024), _tile(e, 2048)
    per_b = s // tm
    row = pl.BlockSpec((1, tm, tn), lambda i, j, k: (i // per_b, i % per_b, j))
    return _mm_call(
        v, w, [x, gate.reshape(b, 1, d)],
        [row, pl.BlockSpec((1, 1, tn), lambda i, j, k: (i // per_b, 0, j))],
        grid=(b * per_b, d // tn, e // tk),
        a_spec=pl.BlockSpec((tm, tk), lambda i, j, k: (i, k)),
        w_spec=pl.BlockSpec((tk, tn), lambda i, j, k: (k, j)),
        out_spec=row,
        out_shape=(b, s, d), out_dtype=F32, epilogue=_res_epilogue, name="out_proj")


def _out_proj_tmajor(v, w, x, gate):
    b, s, d = x.shape
    e = v.shape[1]
    c = s // S5_T
    tn, tk = _tile(d, 1024), _tile(e, 2048)
    per_t = d // tn
    row = pl.BlockSpec((b, c, tn), lambda t, j, k: (0, 0, t * per_t + j))
    out = _mm_call(
        v, w, [x.reshape(b, c, S5_T * d), gate.reshape(b, 1, d)],
        [row, pl.BlockSpec((b, 1, tn), lambda t, j, k: (0, 0, j))],
        grid=(S5_T, d // tn, e // tk),
        a_spec=pl.BlockSpec((b * c, tk), lambda t, j, k: (t, k)),
        w_spec=pl.BlockSpec((tk, tn), lambda t, j, k: (k, j)),
        out_spec=row,
        out_shape=(b, c, S5_T * d), out_dtype=F32, epilogue=_res_epilogue, name="out_proj_tmajor")
    return out.reshape(b, s, d)


def _pool_epilogue(acc, scale, z):
    return acc * scale * jax.nn.silu(z.astype(F32))


def _pool_proj(delta, uz, w, scale):
    m, e = delta.shape
    ng, dg, _ = w.shape
    tm, tn = _tile(m, 1024), _tile(dg, 1024)
    per_g = dg // tn
    zoff = e // tn
    return _mm_call(
        delta, w.reshape(ng * dg, dg), [scale.reshape(1, e), uz],
        [pl.BlockSpec((1, tn), lambda i, j, k: (0, j)),
         pl.BlockSpec((tm, tn), lambda i, j, k: (i, j + zoff))],
        grid=(m // tm, e // tn, 1),
        a_spec=pl.BlockSpec((tm, dg), lambda i, j, k: (i, j // per_g)),
        w_spec=pl.BlockSpec((dg, tn), lambda i, j, k: (j // per_g, j % per_g)),
        out_spec=pl.BlockSpec((tm, tn), lambda i, j, k: (i, j)),
        out_shape=(m, e), out_dtype=BF16, epilogue=_pool_epilogue, name="pool_proj")


def _s5_pitch(rows):
    pitch = -(-rows // SUBLANES) * SUBLANES
    if (pitch // SUBLANES) % 2 == 0:
        pitch += SUBLANES
    return pitch


def _swap_blocks(vs, width):
    n = len(vs)
    lane = lax.broadcasted_iota(jnp.int32, vs[0].shape, 1)
    total = n * width
    delta = n // 2
    while delta:
        upper = (lane & (delta * width)) != 0
        nxt = list(vs)
        for a in range(n):
            if a & delta:
                continue
            b = a + delta
            nxt[a] = jnp.where(upper, pltpu.roll(vs[b], delta * width, 1), vs[a])
            nxt[b] = jnp.where(upper, vs[b], pltpu.roll(vs[a], total - delta * width, 1))
        vs = nxt
        delta //= 2
    return vs


def _s5_kernel(x_ref, wm_ref, ws_ref, wo_ref, coef_ref, y_ref, u_ref, s_ref, *, n_ctx, n_lat):
    gb, rows_all, _ = u_ref.shape
    nb = x_ref.shape[1]
    n_seq = n_lat + n_ctx
    pitch = s_ref.shape[1] // gb
    jch = LANES // gb

    for half in range(S5_T // gb):
        xs = [pltpu.bitcast(x_ref[half * gb + t].reshape(rows_all, LANES), jnp.uint32) for t in range(gb)]
        for g, v in enumerate(_swap_blocks(xs, jch)):
            u_ref[g, :, half * LANES:(half + 1) * LANES] = pltpu.bitcast(v, BF16)

    for g in range(gb):
        s = jnp.dot(u_ref[g], ws_ref[g], preferred_element_type=F32)
        for q in range(4):
            s_ref[q, g * pitch:g * pitch + rows_all, :] = s[:, q * LANES:(q + 1) * LANES]

    ar_f, ai_f, ar_r, ai_r = coef_ref[0], coef_ref[1], coef_ref[2], coef_ref[3]

    def rows(slab, row):
        return s_ref.at[slab][pl.ds(row, gb, stride=pitch), :]

    def step(i, carry):
        rf = jnp.where(i < n_ctx, n_lat + i, i - n_ctx)
        rr = n_seq - 1 - i
        out = []
        for b in range(nb):
            hf, hfs, hr, hrs = carry[4 * b:4 * b + 4]
            row_f = b * n_seq + rf
            row_r = b * n_seq + rr
            sf, sfs = rows(0, row_f), rows(2, row_f)
            sr, srs = rows(1, row_r), rows(3, row_r)
            s_ref.at[0][pl.ds(row_f, gb, stride=pitch), :] = hf
            s_ref.at[1][pl.ds(row_r, gb, stride=pitch), :] = hr
            out += [ar_f * hf + ai_f * hfs + sf, ar_f * hfs - ai_f * hf + sfs,
                    ar_r * hr + ai_r * hrs + sr, ar_r * hrs - ai_r * hr + srs]
        return tuple(out)

    zero = jnp.zeros((gb, LANES), F32)
    lax.fori_loop(0, n_seq, step, (zero,) * (4 * nb))

    for g in range(gb):
        for b in range(nb):
            lo = b * n_seq
            h = jnp.concatenate([s_ref[0, g * pitch + lo:g * pitch + lo + n_lat, :],
                                 s_ref[1, g * pitch + lo:g * pitch + lo + n_lat, :]], axis=1)
            acc = jnp.dot(u_ref[g, lo:lo + n_lat, :], wm_ref[g], preferred_element_type=F32)
            acc += jnp.dot(h.astype(BF16), wo_ref[g], preferred_element_type=F32)
            u_ref[g, lo:lo + n_lat, :] = jax.nn.gelu(acc).astype(BF16)

    for half in range(S5_T // gb):
        ys = [pltpu.bitcast(u_ref[g, :, half * LANES:(half + 1) * LANES], jnp.uint32) for g in range(gb)]
        for t, v in enumerate(_swap_blocks(ys, jch)):
            v = pltpu.bitcast(v, BF16)
            for b in range(nb):
                y_ref[half * gb + t, b] = v[b * n_seq:b * n_seq + n_lat]


def _s5_call(ut, wm, ws, wo, coef, *, n_ctx, n_lat):
    t, nb, n_seq, _ = ut.shape
    g, width, _ = wm.shape
    gb = SUBLANES
    rows = nb * n_seq
    assert t == S5_T and n_seq == n_ctx + n_lat and g % gb == 0 and width == 2 * LANES
    assert rows % (2 * SUBLANES) == 0 and n_lat % (2 * SUBLANES) == 0
    pitch = _s5_pitch(rows)
    grp = lambda shape: pl.BlockSpec((gb,) + shape, lambda i: (i, 0, 0))
    return pl.pallas_call(
        functools.partial(_s5_kernel, n_ctx=n_ctx, n_lat=n_lat),
        grid=(g // gb,),
        in_specs=[pl.BlockSpec((t, nb, n_seq, LANES), lambda i: (0, 0, 0, i)),
                  grp((width, width)), grp((width, 4 * LANES)), grp((width, width)),
                  pl.BlockSpec((4, gb, LANES), lambda i: (0, i, 0))],
        out_specs=pl.BlockSpec((t, nb, n_lat, LANES), lambda i: (0, 0, 0, i)),
        out_shape=jax.ShapeDtypeStruct((t, nb, n_lat, g * width // t), BF16),
        scratch_shapes=[pltpu.VMEM((gb, rows, width), BF16),
                        pltpu.VMEM((4, gb * pitch, LANES), F32)],
        compiler_params=_params(("parallel",)),
        name="s5_scan",
    )(ut, wm, ws, wo, coef)


def _s5_weights(lam_re, lam_im, log_step, b_re, b_im, c_re, c_im, d_skip):
    t = S5_T
    _, g, p, j = b_re.shape
    hi = lax.Precision.HIGHEST
    lam = lax.complex(lam_re.astype(F32), lam_im.astype(F32))
    dt = jnp.exp(log_step.astype(F32))[..., None]
    lam_bar = jnp.exp(lam * dt)
    bb = ((lam_bar - 1.0) / lam)[..., None] * lax.complex(b_re.astype(F32), b_im.astype(F32))
    cm = lax.complex(c_re.astype(F32), c_im.astype(F32))
    steps = jnp.arange(t + 1, dtype=F32)[:, None, None, None]
    pw = jnp.exp(lam * dt * steps)

    z = cm[None] * pw[:t, :, :, None, :]
    kern = (jnp.einsum('tdgjp,dgpi->dtgji', z.real, bb.real, precision=hi)
            - jnp.einsum('tdgjp,dgpi->dtgji', z.imag, bb.imag, precision=hi))
    tau = jnp.arange(t)[None, :] - jnp.arange(t)[:, None]
    fwd = jnp.where((tau >= 0)[:, :, None, None, None], kern[0][jnp.clip(tau, 0, t - 1)], 0.0)
    rev = jnp.where((tau <= 0)[:, :, None, None, None], kern[1][jnp.clip(-tau, 0, t - 1)], 0.0)
    skip = (jnp.eye(t, dtype=F32)[:, :, None, None, None] * jnp.eye(j, dtype=F32)[None, None, None]
            * d_skip.astype(F32).reshape(g, j)[None, None, :, :, None])
    wm = (fwd + rev + skip).transpose(2, 0, 4, 1, 3).reshape(g, t * j, t * j)

    xf = (pw[t - 1::-1, 0][:, :, :, None] * bb[0][None]).transpose(1, 0, 3, 2)
    xr = (pw[:t, 1][:, :, :, None] * bb[1][None]).transpose(1, 0, 3, 2)
    ws = jnp.concatenate([xf.real, xf.imag, xr.real, xr.imag, xf.imag, xf.real, xr.imag, xr.real],
                         axis=-1).reshape(g, t * j, 8 * p)

    zf = (cm[0][None] * pw[1:, 0][:, :, None, :]).transpose(1, 3, 0, 2)
    zr = (cm[1][None] * pw[t:0:-1, 1][:, :, None, :]).transpose(1, 3, 0, 2)
    wo = jnp.concatenate([zf.real, -zf.imag, zr.real, -zr.imag], axis=1).reshape(g, 4 * p, t * j)

    a = pw[t]
    coef = jnp.stack([jnp.concatenate([a[0].real, a[0].real], -1), jnp.concatenate([-a[0].imag, a[0].imag], -1),
                      jnp.concatenate([a[1].real, a[1].real], -1), jnp.concatenate([-a[1].imag, a[1].imag], -1)])
    return wm.astype(BF16), ws.astype(BF16), wo.astype(BF16), coef


def _pool_kernel(u_ref, pc_ref, invc_ref, o_ref, cs_ref, *, n_rows, per_group):
    slab = pc_ref.shape[1]
    group = pl.program_id(1) // per_group

    def run(w):
        for s in range(n_rows * GRID_W // slab):
            sl = slice(s * slab, (s + 1) * slab)
            cs_ref[sl, :] = jnp.dot(pc_ref[0], u_ref[0, sl, :], preferred_element_type=F32)
        lo, hi = w // 2, w - w // 2
        inv_c = invc_ref[0]

        def grid_row(r):
            return pl.ds(pl.multiple_of(r * GRID_W, GRID_W), GRID_W)

        def emit(r, taps):
            acc = cs_ref[grid_row(r + taps[0]), :]
            for dr in taps[1:]:
                acc += cs_ref[grid_row(r + dr), :]
            own = grid_row(r)
            mean = acc * (inv_c * (1.0 / len(taps)))
            o_ref[0, own, :] = (mean - u_ref[0, own, :].astype(F32)).astype(o_ref.dtype)

        full = list(range(-lo, hi))
        first_full, last_full = lo, n_rows - hi
        for r in list(range(first_full)) + list(range(last_full + 1, n_rows)):
            emit(r, [dr for dr in full if 0 <= r + dr < n_rows])

        def body(r, carry):
            emit(r, full)
            return carry

        lax.fori_loop(first_full, last_full + 1, body, 0)

    for k, w in enumerate(POOL_WINDOWS):
        pl.when(group == k)(functools.partial(run, w))


def _pool_consts(slab):
    pcs, invs = [], []
    pos = np.arange(GRID_W)
    for w in POOL_WINDOWS:
        lo, hi = np.clip(pos - w // 2, 0, GRID_W), np.clip(pos + w - w // 2, 0, GRID_W)
        inside = (pos[None, :] >= lo[:, None]) & (pos[None, :] < hi[:, None])
        pcs.append(np.kron(np.eye(slab // GRID_W), inside.astype(np.float32)))
        invs.append((1.0 / (hi - lo)).astype(np.float32).reshape(GRID_W, 1))
    return jnp.asarray(np.stack(pcs), BF16), jnp.asarray(np.stack(invs), F32)


def _pool_call(uz, e):
    b, l, _ = uz.shape
    n_rows = l // GRID_W
    dg = e // len(POOL_WINDOWS)
    tc = _tile(dg, 256)
    slab = _tile(l, 256)
    pc, invc = _pool_consts(slab)
    per_group = dg // tc
    blk = pl.BlockSpec((1, l, tc), lambda i, j: (i, 0, j))
    return pl.pallas_call(
        functools.partial(_pool_kernel, n_rows=n_rows, per_group=per_group),
        grid=(b, e // tc),
        in_specs=[blk,
                  pl.BlockSpec((1, slab, slab), lambda i, j: (j // per_group, 0, 0)),
                  pl.BlockSpec((1, GRID_W, 1), lambda i, j: (j // per_group, 0, 0))],
        out_specs=blk,
        out_shape=jax.ShapeDtypeStruct((b, l, e), BF16),
        scratch_shapes=[pltpu.VMEM((l, tc), F32)],
        compiler_params=_params(("parallel", "parallel")),
        name="pool_delta",
    )(uz, pc, invc)


def kernel(x, c, ctx, c_ctx, norm_w, w_ada, b_ada, w_in, w_out, s5_lam_re, s5_lam_im, s5_log_step,
           s5_b_re, s5_b_im, s5_c_re, s5_c_im, s5_d, s5_w_glu, s5_b_glu, pool_w, pool_scale,
           final_norm_w):
    bsz, seq, d = x.shape
    n_ctx_tok = ctx.shape[1]
    e = w_in.shape[2] // 2
    jch = s5_b_re.shape[4]
    assert bsz + 1 <= SUBLANES and S5_T * jch == 2 * LANES and 2 * s5_b_re.shape[3] == LANES
    assert seq % S5_T == 0 and n_ctx_tok % S5_T == 0 and seq % GRID_W == 0

    cond = jnp.zeros((SUBLANES, d), F32).at[:bsz].set(c).at[bsz].set(c_ctx)
    mods = _ada_call(cond, w_ada, b_ada)
    shift, scale, gate = mods[..., :d], mods[..., d:2 * d], mods[..., 2 * d:]

    ctx_mod = lambda v: jnp.broadcast_to(v[0, bsz], (bsz, d))
    h = _norm_call(x, norm_w[0], scale[0, :bsz], shift[0, :bsz], out_dtype=BF16, out_rows=seq + n_ctx_tok)
    h = _norm_call(ctx, norm_w[0], ctx_mod(scale), ctx_mod(shift), out_dtype=BF16,
                   out_rows=seq + n_ctx_tok, row_offset=seq, into=h)
    uz = _in_proj_tmajor(h, w_in[0].astype(BF16))
    n_ctx, n_lat = n_ctx_tok // S5_T, seq // S5_T
    wm, ws, wo, coef = _s5_weights(s5_lam_re[0], s5_lam_im[0], s5_log_step[0], s5_b_re[0], s5_b_im[0],
                                   s5_c_re[0], s5_c_im[0], s5_d[0])
    y = _s5_call(uz, wm, ws, wo, coef, n_ctx=n_ctx, n_lat=n_lat)
    v = _glu_gate(y.reshape(S5_T * bsz * n_lat, e), uz, s5_w_glu[0].astype(BF16), s5_b_glu[0], n_lat)
    x1 = _out_proj_tmajor(v, w_out[0].astype(BF16), x, gate[0, :bsz])

    h = _norm_call(x1, norm_w[1], scale[1, :bsz], shift[1, :bsz], out_dtype=BF16)
    uz = _in_proj(h.reshape(bsz * seq, d), w_in[1].astype(BF16))
    delta = _pool_call(uz.reshape(bsz, seq, 2 * e), e)
    v = _pool_proj(delta.reshape(bsz * seq, e), uz, pool_w[0].astype(BF16), pool_scale[0])
    x2 = _out_proj(v, w_out[1].astype(BF16), x1, gate[1, :bsz])

    return _norm_call(x2, final_norm_w, out_dtype=x.dtype)
```

```python
import functools

import jax
import jax.numpy as jnp
import numpy as np
from jax import lax
from jax.experimental import pallas as pl
from jax.experimental.pallas import tpu as pltpu

F32 = jnp.float32
BF16 = jnp.bfloat16

GRID_W = 64
POOL_WINDOWS = (2, 4, 8, 16)
RMS_EPS = 1e-6
S5_T = 16
LANES = 128
SUBLANES = 8
VMEM_LIMIT = 56 * 1024 * 1024


def _params(sem):
    return pltpu.CompilerParams(dimension_semantics=sem, vmem_limit_bytes=VMEM_LIMIT)


def _tile(n, pref):
    t = min(n, pref)
    while n % t:
        t //= 2
    return t


def _ada_kernel(c_ref, w_ref, b_ref, o_ref):
    s = jax.nn.silu(c_ref[...]).astype(BF16)
    o_ref[0] = jnp.dot(s, w_ref[0].astype(BF16), preferred_element_type=F32) + b_ref[0]


def _ada_call(cond, w_ada, b_ada):
    depth, d, n = w_ada.shape
    tn = _tile(n, 512)
    return pl.pallas_call(
        _ada_kernel,
        grid=(depth, n // tn),
        in_specs=[
            pl.BlockSpec((SUBLANES, d), lambda i, j: (0, 0)),
            pl.BlockSpec((1, d, tn), lambda i, j: (i, 0, j)),
            pl.BlockSpec((1, 1, tn), lambda i, j: (i, 0, j)),
        ],
        out_specs=pl.BlockSpec((1, SUBLANES, tn), lambda i, j: (i, 0, j)),
        out_shape=jax.ShapeDtypeStruct((depth, SUBLANES, n), F32),
        compiler_params=_params(("parallel", "parallel")),
        name="adaln",
    )(cond, w_ada, b_ada.reshape(depth, 1, n))


def _norm_kernel(*refs, modulate):
    x_ref, w_ref = refs[0], refs[1]
    o_ref = refs[-1]
    x = x_ref[0]
    y = x * lax.rsqrt(jnp.mean(x * x, axis=-1, keepdims=True) + RMS_EPS) * w_ref[...]
    if modulate:
        scale_ref, shift_ref = refs[2], refs[3]
        y = y * (1.0 + scale_ref[0]) + shift_ref[0]
    o_ref[0] = y.astype(o_ref.dtype)


def _norm_call(x, w, scale=None, shift=None, *, out_dtype, out_rows=None, row_offset=0, into=None):
    b, s, d = x.shape
    out_rows = s if out_rows is None else out_rows
    tm = _tile(s, 256)
    assert row_offset % tm == 0
    off = row_offset // tm
    modulate = scale is not None
    in_specs = [pl.BlockSpec((1, tm, d), lambda i, j: (i, j, 0)), pl.BlockSpec((1, d), lambda i, j: (0, 0))]
    args = [x, w.reshape(1, d)]
    if modulate:
        per_batch = pl.BlockSpec((1, 1, d), lambda i, j: (i, 0, 0))
        in_specs += [per_batch, per_batch]
        args += [scale.reshape(b, 1, d), shift.reshape(b, 1, d)]
    aliases = {}
    if into is not None:
        aliases = {len(args): 0}
        in_specs.append(pl.BlockSpec(memory_space=pl.ANY))
        args.append(into)
    return pl.pallas_call(
        functools.partial(_norm_kernel, modulate=modulate),
        grid=(b, s // tm),
        in_specs=in_specs,
        out_specs=pl.BlockSpec((1, tm, d), lambda i, j: (i, j + off, 0)),
        out_shape=jax.ShapeDtypeStruct((b, out_rows, d), out_dtype),
        input_output_aliases=aliases,
        compiler_params=_params(("parallel", "parallel")),
        name="rmsnorm_mod" if modulate else "rmsnorm",
    )(*args)


def _mm_kernel(*refs, nk, n_extra, epilogue):
    a_ref, w_ref = refs[0], refs[1]
    extras = refs[2:2 + n_extra]
    o_ref = refs[2 + n_extra]
    a = a_ref[...]
    part = jnp.dot(a.reshape(-1, a.shape[-1]), w_ref[...], preferred_element_type=F32)

    def finish(acc):
        out = epilogue(acc, *[e[...] for e in extras])
        o_ref[...] = out.reshape(o_ref.shape).astype(o_ref.dtype)

    if nk == 1:
        finish(part)
        return
    acc_ref = refs[3 + n_extra]
    k = pl.program_id(2)

    @pl.when(k == 0)
    def _():
        acc_ref[...] = part

    @pl.when(jnp.logical_and(k > 0, k < nk - 1))
    def _():
        acc_ref[...] += part

    @pl.when(k == nk - 1)
    def _():
        finish(acc_ref[...] + part)


def _mm_call(a, w, extras, extra_specs, *, grid, a_spec, w_spec, out_spec, out_shape, out_dtype,
             epilogue, name):
    nk = grid[2]
    blk = [n for n in out_spec.block_shape if n is not None]
    scratch = [pltpu.VMEM((int(np.prod(blk[:-1])), blk[-1]), F32)] if nk > 1 else []
    return pl.pallas_call(
        functools.partial(_mm_kernel, nk=nk, n_extra=len(extras), epilogue=epilogue),
        grid=grid,
        in_specs=[a_spec, w_spec, *extra_specs],
        out_specs=out_spec,
        out_shape=jax.ShapeDtypeStruct(out_shape, out_dtype),
        scratch_shapes=scratch,
        compiler_params=_params(("parallel", "parallel", "arbitrary")),
        name=name,
    )(a, w, *extras)


def _in_proj(h, w):
    m, d = h.shape
    n = w.shape[1]
    tm, tn = _tile(m, 1024), _tile(n, 1024)
    return _mm_call(
        h, w, [], [],
        grid=(m // tm, n // tn, 1),
        a_spec=pl.BlockSpec((tm, d), lambda i, j, k: (i, 0)),
        w_spec=pl.BlockSpec((d, tn), lambda i, j, k: (0, j)),
        out_spec=pl.BlockSpec((tm, tn), lambda i, j, k: (i, j)),
        out_shape=(m, n), out_dtype=BF16, epilogue=lambda acc: acc, name="in_proj")


def _in_proj_tmajor(h, w):
    b, s, d = h.shape
    n = w.shape[1]
    c = s // S5_T
    tn = _tile(n, 1024)
    return _mm_call(
        h.reshape(b, c, S5_T * d), w, [], [],
        grid=(S5_T, n // tn, 1),
        a_spec=pl.BlockSpec((b, c, d), lambda t, j, k: (0, 0, t)),
        w_spec=pl.BlockSpec((d, tn), lambda t, j, k: (0, j)),
        out_spec=pl.BlockSpec((None, b, c, tn), lambda t, j, k: (t, 0, 0, j)),
        out_shape=(S5_T, b, c, n), out_dtype=BF16, epilogue=lambda acc: acc, name="in_proj_tmajor")


def _glu_epilogue(acc, bias, y, z):
    y = y.astype(F32)
    z = z.astype(F32).reshape(acc.shape)
    return y * jax.nn.sigmoid(acc + bias) * jax.nn.silu(z)


def _glu_gate(y, uz, w, bias, n_lat):
    m, e = y.shape
    t, b = uz.shape[0], uz.shape[1]
    tm = b * n_lat
    assert m == t * tm
    tn, tk = _tile(e, 1024), _tile(e, 2048)
    zoff = e // tn
    return _mm_call(
        y, w, [bias.reshape(1, e), y, uz],
        [pl.BlockSpec((1, tn), lambda i, j, k: (0, j)),
         pl.BlockSpec((tm, tn), lambda i, j, k: (i, j)),
         pl.BlockSpec((None, b, n_lat, tn), lambda i, j, k: (i, 0, 0, j + zoff))],
        grid=(t, e // tn, e // tk),
        a_spec=pl.BlockSpec((tm, tk), lambda i, j, k: (i, k)),
        w_spec=pl.BlockSpec((tk, tn), lambda i, j, k: (k, j)),
        out_spec=pl.BlockSpec((tm, tn), lambda i, j, k: (i, j)),
        out_shape=(m, e), out_dtype=BF16, epilogue=_glu_epilogue, name="s5_glu")


def _res_epilogue(acc, x, gate):
    return x + gate * acc.reshape(x.shape)


def _out_proj(v, w, x, gate):
    b, s, d = x.shape
    e = v.shape[1]
    tm, tn, tk = _tile(s, 1024), _tile(d, 1024), _tile(e, 2048)
    per_b = s // tm
    row = pl.BlockSpec((1, tm, tn), lambda i, j, k: (i // per_b, i % per_b, j))
    return _mm_call(
        v, w, [x, gate.reshape(b, 1, d)],
        [row, pl.BlockSpec((1, 1, tn), lambda i, j, k: (i // per_b, 0, j))],
        grid=(b * per_b, d // tn, e // tk),
        a_spec=pl.BlockSpec((tm, tk), lambda i, j, k: (i, k)),
        w_spec=pl.BlockSpec((tk, tn), lambda i, j, k: (k, j)),
        out_spec=row,
        out_shape=(b, s, d), out_dtype=F32, epilogue=_res_epilogue, name="out_proj")


def _out_proj_tmajor(v, w, x, gate):
    b, s, d = x.shape
    e = v.shape[1]
    c = s // S5_T
    tn, tk = _tile(d, 1024), _tile(e, 2048)
    per_t = d // tn
    row = pl.BlockSpec((b, c, tn), lambda t, j, k: (0, 0, t * per_t + j))
    out = _mm_call(
        v, w, [x.reshape(b, c, S5_T * d), gate.reshape(b, 1, d)],
        [row, pl.BlockSpec((b, 1, tn), lambda t, j, k: (0, 0, j))],
        grid=(S5_T, d // tn, e // tk),
        a_spec=pl.BlockSpec((b * c, tk), lambda t, j, k: (t, k)),
        w_spec=pl.BlockSpec((tk, tn), lambda t, j, k: (k, j)),
        out_spec=row,
        out_shape=(b, c, S5_T * d), out_dtype=F32, epilogue=_res_epilogue, name="out_proj_tmajor")
    return out.reshape(b, s, d)


def _pool_epilogue(acc, scale, z):
    return acc * scale * jax.nn.silu(z.astype(F32))


def _pool_proj(delta, uz, w, scale):
    m, e = delta.shape
    ng, dg, _ = w.shape
    tm, tn = _tile(m, 1024), _tile(dg, 1024)
    per_g = dg // tn
    zoff = e // tn
    return _mm_call(
        delta, w.reshape(ng * dg, dg), [scale.reshape(1, e), uz],
        [pl.BlockSpec((1, tn), lambda i, j, k: (0, j)),
         pl.BlockSpec((tm, tn), lambda i, j, k: (i, j + zoff))],
        grid=(m // tm, e // tn, 1),
        a_spec=pl.BlockSpec((tm, dg), lambda i, j, k: (i, j // per_g)),
        w_spec=pl.BlockSpec((dg, tn), lambda i, j, k: (j // per_g, j % per_g)),
        out_spec=pl.BlockSpec((tm, tn), lambda i, j, k: (i, j)),
        out_shape=(m, e), out_dtype=BF16, epilogue=_pool_epilogue, name="pool_proj")


def _s5_pitch(rows):
    pitch = -(-rows // SUBLANES) * SUBLANES
    if (pitch // SUBLANES) % 2 == 0:
        pitch += SUBLANES
    return pitch


def _swap_blocks(vs, width):
    n = len(vs)
    lane = lax.broadcasted_iota(jnp.int32, vs[0].shape, 1)
    total = n * width
    delta = n // 2
    while delta:
        upper = (lane & (delta * width)) != 0
        nxt = list(vs)
        for a in range(n):
            if a & delta:
                continue
            b = a + delta
            nxt[a] = jnp.where(upper, pltpu.roll(vs[b], delta * width, 1), vs[a])
            nxt[b] = jnp.where(upper, vs[b], pltpu.roll(vs[a], total - delta * width, 1))
        vs = nxt
        delta //= 2
    return vs


def _s5_kernel(x_ref, wm_ref, ws_ref, wo_ref, coef_ref, y_ref, u_ref, s_ref, *, n_ctx, n_lat):
    gb, rows_all, _ = u_ref.shape
    nb = x_ref.shape[1]
    n_seq = n_lat + n_ctx
    pitch = s_ref.shape[1] // gb
    jch = LANES // gb

    for half in range(S5_T // gb):
        xs = [pltpu.bitcast(x_ref[half * gb + t].reshape(rows_all, LANES), jnp.uint32) for t in range(gb)]
        for g, v in enumerate(_swap_blocks(xs, jch)):
            u_ref[g, :, half * LANES:(half + 1) * LANES] = pltpu.bitcast(v, BF16)

    for g in range(gb):
        s = jnp.dot(u_ref[g], ws_ref[g], preferred_element_type=F32)
        for q in range(4):
            s_ref[q, g * pitch:g * pitch + rows_all, :] = s[:, q * LANES:(q + 1) * LANES]

    ar_f, ai_f, ar_r, ai_r = coef_ref[0], coef_ref[1], coef_ref[2], coef_ref[3]

    def rows(slab, row):
        return s_ref.at[slab][pl.ds(row, gb, stride=pitch), :]

    def step(i, carry):
        rf = jnp.where(i < n_ctx, n_lat + i, i - n_ctx)
        rr = n_seq - 1 - i
        out = []
        for b in range(nb):
            hf, hfs, hr, hrs = carry[4 * b:4 * b + 4]
            row_f = b * n_seq + rf
            row_r = b * n_seq + rr
            sf, sfs = rows(0, row_f), rows(2, row_f)
            sr, srs = rows(1, row_r), rows(3, row_r)
            s_ref.at[0][pl.ds(row_f, gb, stride=pitch), :] = hf
            s_ref.at[1][pl.ds(row_r, gb, stride=pitch), :] = hr
            out += [ar_f * hf + ai_f * hfs + sf, ar_f * hfs - ai_f * hf + sfs,
                    ar_r * hr + ai_r * hrs + sr, ar_r * hrs - ai_r * hr + srs]
        return tuple(out)

    zero = jnp.zeros((gb, LANES), F32)
    lax.fori_loop(0, n_seq, step, (zero,) * (4 * nb))

    for g in range(gb):
        for b in range(nb):
            lo = b * n_seq
            h = jnp.concatenate([s_ref[0, g * pitch + lo:g * pitch + lo + n_lat, :],
                                 s_ref[1, g * pitch + lo:g * pitch + lo + n_lat, :]], axis=1)
            acc = jnp.dot(u_ref[g, lo:lo + n_lat, :], wm_ref[g], preferred_element_type=F32)
            acc += jnp.dot(h.astype(BF16), wo_ref[g], preferred_element_type=F32)
            u_ref[g, lo:lo + n_lat, :] = jax.nn.gelu(acc).astype(BF16)

    for half in range(S5_T // gb):
        ys = [pltpu.bitcast(u_ref[g, :, half * LANES:(half + 1) * LANES], jnp.uint32) for g in range(gb)]
        for t, v in enumerate(_swap_blocks(ys, jch)):
            v = pltpu.bitcast(v, BF16)
            for b in range(nb):
                y_ref[half * gb + t, b] = v[b * n_seq:b * n_seq + n_lat]


def _s5_call(ut, wm, ws, wo, coef, *, n_ctx, n_lat):
    t, nb, n_seq, _ = ut.shape
    g, width, _ = wm.shape
    gb = SUBLANES
    rows = nb * n_seq
    assert t == S5_T and n_seq == n_ctx + n_lat and g % gb == 0 and width == 2 * LANES
    assert rows % (2 * SUBLANES) == 0 and n_lat % (2 * SUBLANES) == 0
    pitch = _s5_pitch(rows)
    grp = lambda shape: pl.BlockSpec((gb,) + shape, lambda i: (i, 0, 0))
    return pl.pallas_call(
        functools.partial(_s5_kernel, n_ctx=n_ctx, n_lat=n_lat),
        grid=(g // gb,),
        in_specs=[pl.BlockSpec((t, nb, n_seq, LANES), lambda i: (0, 0, 0, i)),
                  grp((width, width)), grp((width, 4 * LANES)), grp((width, width)),
                  pl.BlockSpec((4, gb, LANES), lambda i: (0, i, 0))],
        out_specs=pl.BlockSpec((t, nb, n_lat, LANES), lambda i: (0, 0, 0, i)),
        out_shape=jax.ShapeDtypeStruct((t, nb, n_lat, g * width // t), BF16),
        scratch_shapes=[pltpu.VMEM((gb, rows, width), BF16),
                        pltpu.VMEM((4, gb * pitch, LANES), F32)],
        compiler_params=_params(("parallel",)),
        name="s5_scan",
    )(ut, wm, ws, wo, coef)


def _s5_weights(lam_re, lam_im, log_step, b_re, b_im, c_re, c_im, d_skip):
    _, g, p, j = b_re.shape
    t = S5_T
    gb = SUBLANES
    assert 2 * p == LANES and t * j == 2 * LANES and g % gb == 0
    twice = lambda a: jnp.concatenate([a, a], axis=-1).astype(F32)
    lam_re2, lam_im2 = twice(lam_re), twice(lam_im)
    ls2 = jnp.broadcast_to(log_step.astype(F32)[..., None], (2, g, 2 * p))
    b_re2, b_im2 = twice(b_re.transpose(0, 1, 3, 2)), twice(b_im.transpose(0, 1, 3, 2))
    c_re2, c_im2 = twice(c_re), twice(c_im)
    d_lane = jnp.tile(d_skip.astype(F32).reshape(g, 1, j), (1, 1, t))

    vec = pl.BlockSpec((2, gb, 2 * p), lambda i: (0, i, 0))
    mat = pl.BlockSpec((2, gb, j, 2 * p), lambda i: (0, i, 0, 0))
    grp = lambda shape: pl.BlockSpec((gb,) + shape, lambda i: (i, 0, 0))
    return pl.pallas_call(
        _s5_weights_kernel,
        grid=(g // gb,),
        in_specs=[vec, vec, vec, mat, mat, mat, mat, grp((1, t * j))],
        out_specs=[grp((t * j, t * j)), grp((t * j, 8 * p)), grp((4 * p, t * j)),
                   pl.BlockSpec((4, gb, 2 * p), lambda i: (0, i, 0))],
        out_shape=[jax.ShapeDtypeStruct((g, t * j, t * j), BF16),
                   jax.ShapeDtypeStruct((g, t * j, 8 * p), BF16),
                   jax.ShapeDtypeStruct((g, 4 * p, t * j), BF16),
                   jax.ShapeDtypeStruct((4, g, 2 * p), F32)],
        scratch_shapes=[pltpu.VMEM((2, (t + 1) * j, 2 * p), F32)],
        compiler_params=_params(("parallel",)),
        name="s5_weights",
    )(lam_re2, lam_im2, ls2, b_re2, b_im2, c_re2, c_im2, d_lane)


def _s5_weights_kernel(lre_ref, lim_ref, ls_ref, bre_ref, bim_ref, cre_ref, cim_ref, d_ref,
                       wm_ref, ws_ref, wo_ref, coef_ref, zt_ref):
    t = S5_T
    gb = wm_ref.shape[0]
    j = bre_ref.shape[2]
    width = t * j
    hi = lax.Precision.HIGHEST
    lo_half = lax.broadcasted_iota(jnp.int32, (1, LANES), 1) < LANES // 2
    pick = lambda first, second: jnp.where(lo_half, first, second)
    step_k = lax.broadcasted_iota(jnp.int32, (3 * SUBLANES, LANES), 0).astype(F32)
    lane_w = lax.broadcasted_iota(jnp.int32, (j, width), 1)
    row_w = lax.broadcasted_iota(jnp.int32, (j, width), 0)
    nt = (((1,), (1,)), ((), ()))

    def one_group(g, carry):
        kts = []
        for d in range(2):
            lr, li = lre_ref[d, pl.ds(g, 1), :], lim_ref[d, pl.ds(g, 1), :]
            dt = jnp.exp(ls_ref[d, pl.ds(g, 1), :])
            mag = jnp.exp(lr * dt * step_k)
            ang = li * dt * step_k
            p_re, p_im = mag * jnp.cos(ang), mag * jnp.sin(ang)
            power = lambda k: (p_re[k:k + 1], p_im[k:k + 1])

            e_re, e_im = power(1)
            den = lr * lr + li * li
            f_re = ((e_re - 1.0) * lr + e_im * li) / den
            f_im = (e_im * lr - (e_re - 1.0) * li) / den
            b_re, b_im = bre_ref[d, g], bim_ref[d, g]
            bb_re, bb_im = f_re * b_re - f_im * b_im, f_re * b_im + f_im * b_re
            bb_a, bb_b = pick(bb_re, bb_im), pick(-bb_im, bb_re)
            bb_c, bb_d = pick(bb_im, bb_re), pick(bb_re, -bb_im)
            c_re, c_im = cre_ref[d, g], cim_ref[d, g]
            cz_a, cz_b = pick(c_re, -c_im), pick(-c_im, -c_re)

            for s in range(t):
                pr, pi = power(t - 1 - s if d == 0 else s)
                rows = slice(s * j, (s + 1) * j)
                ws_ref[g, rows, d * LANES:(d + 1) * LANES] = (pr * bb_a + pi * bb_b).astype(BF16)
                ws_ref[g, rows, (2 + d) * LANES:(3 + d) * LANES] = (pr * bb_c + pi * bb_d).astype(BF16)

            for m in range(t + 1):
                pr, pi = power(m if d == 0 else t - m)
                zt_ref[d, m * j:(m + 1) * j, :] = pr * cz_a + pi * cz_b

            z_kern = zt_ref[d, 0:width, :] if d == 0 else zt_ref[d, j:j + width, :]
            kts.append(lax.dot_general(bb_a, z_kern, nt, precision=hi, preferred_element_type=F32))

            pr, pi = power(t)
            coef_ref[2 * d, pl.ds(g, 1), :] = pr
            coef_ref[2 * d + 1, pl.ds(g, 1), :] = pick(-pi, pi)

        wo_t = jnp.concatenate([zt_ref[0, j:j + width, :], zt_ref[1, 0:width, :]], axis=1)
        wo_ref[g] = wo_t.T.astype(BF16)

        d_lane = d_ref[g]
        for s in range(t):
            fwd = jnp.where(lane_w >= s * j, pltpu.roll(kts[0], s * j, 1), 0.0)
            rev = jnp.where(lane_w < (s + 1) * j, pltpu.roll(kts[1], (width - (t - 1 - s) * j) % width, 1), 0.0)
            skip = jnp.where(lane_w - s * j == row_w, d_lane, 0.0)
            wm_ref[g, s * j:(s + 1) * j, :] = (fwd + rev + skip).astype(BF16)
        return carry

    lax.fori_loop(0, gb, one_group, 0)


def _pool_kernel(u_ref, pc_ref, invc_ref, o_ref, cs_ref, *, n_rows, per_group):
    slab = pc_ref.shape[1]
    group = pl.program_id(1) // per_group

    def run(w):
        for s in range(n_rows * GRID_W // slab):
            sl = slice(s * slab, (s + 1) * slab)
            cs_ref[sl, :] = jnp.dot(pc_ref[0], u_ref[0, sl, :], preferred_element_type=F32)
        lo, hi = w // 2, w - w // 2
        inv_c = invc_ref[0]

        def grid_row(r):
            return pl.ds(pl.multiple_of(r * GRID_W, GRID_W), GRID_W)

        def emit(r, taps):
            acc = cs_ref[grid_row(r + taps[0]), :]
            for dr in taps[1:]:
                acc += cs_ref[grid_row(r + dr), :]
            own = grid_row(r)
            mean = acc * (inv_c * (1.0 / len(taps)))
            o_ref[0, own, :] = (mean - u_ref[0, own, :].astype(F32)).astype(o_ref.dtype)

        full = list(range(-lo, hi))
        first_full, last_full = lo, n_rows - hi
        for r in list(range(first_full)) + list(range(last_full + 1, n_rows)):
            emit(r, [dr for dr in full if 0 <= r + dr < n_rows])

        def body(r, carry):
            emit(r, full)
            return carry

        lax.fori_loop(first_full, last_full + 1, body, 0)

    for k, w in enumerate(POOL_WINDOWS):
        pl.when(group == k)(functools.partial(run, w))


def _pool_consts(slab):
    pcs, invs = [], []
    pos = np.arange(GRID_W)
    for w in POOL_WINDOWS:
        lo, hi = np.clip(pos - w // 2, 0, GRID_W), np.clip(pos + w - w // 2, 0, GRID_W)
        inside = (pos[None, :] >= lo[:, None]) & (pos[None, :] < hi[:, None])
        pcs.append(np.kron(np.eye(slab // GRID_W), inside.astype(np.float32)))
        invs.append((1.0 / (hi - lo)).astype(np.float32).reshape(GRID_W, 1))
    return jnp.asarray(np.stack(pcs), BF16), jnp.asarray(np.stack(invs), F32)


def _pool_call(uz, e):
    b, l, _ = uz.shape
    n_rows = l // GRID_W
    dg = e // len(POOL_WINDOWS)
    tc = _tile(dg, 256)
    slab = _tile(l, 256)
    pc, invc = _pool_consts(slab)
    per_group = dg // tc
    blk = pl.BlockSpec((1, l, tc), lambda i, j: (i, 0, j))
    return pl.pallas_call(
        functools.partial(_pool_kernel, n_rows=n_rows, per_group=per_group),
        grid=(b, e // tc),
        in_specs=[blk,
                  pl.BlockSpec((1, slab, slab), lambda i, j: (j // per_group, 0, 0)),
                  pl.BlockSpec((1, GRID_W, 1), lambda i, j: (j // per_group, 0, 0))],
        out_specs=blk,
        out_shape=jax.ShapeDtypeStruct((b, l, e), BF16),
        scratch_shapes=[pltpu.VMEM((l, tc), F32)],
        compiler_params=_params(("parallel", "parallel")),
        name="pool_delta",
    )(uz, pc, invc)


def kernel(x, c, ctx, c_ctx, norm_w, w_ada, b_ada, w_in, w_out, s5_lam_re, s5_lam_im, s5_log_step,
           s5_b_re, s5_b_im, s5_c_re, s5_c_im, s5_d, s5_w_glu, s5_b_glu, pool_w, pool_scale,
           final_norm_w):
    bsz, seq, d = x.shape
    n_ctx_tok = ctx.shape[1]
    e = w_in.shape[2] // 2
    jch = s5_b_re.shape[4]
    assert bsz + 1 <= SUBLANES and S5_T * jch == 2 * LANES and 2 * s5_b_re.shape[3] == LANES
    assert seq % S5_T == 0 and n_ctx_tok % S5_T == 0 and seq % GRID_W == 0

    cond = jnp.zeros((SUBLANES, d), F32).at[:bsz].set(c).at[bsz].set(c_ctx)
    mods = _ada_call(cond, w_ada, b_ada)
    shift, scale, gate = mods[..., :d], mods[..., d:2 * d], mods[..., 2 * d:]

    ctx_mod = lambda v: jnp.broadcast_to(v[0, bsz], (bsz, d))
    h = _norm_call(x, norm_w[0], scale[0, :bsz], shift[0, :bsz], out_dtype=BF16, out_rows=seq + n_ctx_tok)
    h = _norm_call(ctx, norm_w[0], ctx_mod(scale), ctx_mod(shift), out_dtype=BF16,
                   out_rows=seq + n_ctx_tok, row_offset=seq, into=h)
    uz = _in_proj_tmajor(h, w_in[0].astype(BF16))
    n_ctx, n_lat = n_ctx_tok // S5_T, seq // S5_T
    wm, ws, wo, coef = _s5_weights(s5_lam_re[0], s5_lam_im[0], s5_log_step[0], s5_b_re[0], s5_b_im[0],
                                   s5_c_re[0], s5_c_im[0], s5_d[0])
    y = _s5_call(uz, wm, ws, wo, coef, n_ctx=n_ctx, n_lat=n_lat)
    v = _glu_gate(y.reshape(S5_T * bsz * n_lat, e), uz, s5_w_glu[0].astype(BF16), s5_b_glu[0], n_lat)
    x1 = _out_proj_tmajor(v, w_out[0].astype(BF16), x, gate[0, :bsz])

    h = _norm_call(x1, norm_w[1], scale[1, :bsz], shift[1, :bsz], out_dtype=BF16)
    uz = _in_proj(h.reshape(bsz * seq, d), w_in[1].astype(BF16))
    delta = _pool_call(uz.reshape(bsz, seq, 2 * e), e)
    v = _pool_proj(delta.reshape(bsz * seq, e), uz, pool_w[0].astype(BF16), pool_scale[0])
    x2 = _out_proj(v, w_out[1].astype(BF16), x1, gate[1, :bsz])

    return _norm_call(x2, final_norm_w, out_dtype=x.dtype)
```

```python
import functools

import jax
import jax.numpy as jnp
import numpy as np
from jax import lax
from jax.experimental import pallas as pl
from jax.experimental.pallas import tpu as pltpu

F32 = jnp.float32
BF16 = jnp.bfloat16

GRID_W = 64
POOL_WINDOWS = (2, 4, 8, 16)
RMS_EPS = 1e-6
S5_T = 16
LANES = 128
SUBLANES = 8
VMEM_LIMIT = 56 * 1024 * 1024


def _params(sem):
    return pltpu.CompilerParams(dimension_semantics=sem, vmem_limit_bytes=VMEM_LIMIT)


def _tile(n, pref):
    t = min(n, pref)
    while n % t:
        t //= 2
    return t


def _ada_kernel(c_ref, w_ref, b_ref, o_ref):
    s = jax.nn.silu(c_ref[...]).astype(BF16)
    o_ref[0] = jnp.dot(s, w_ref[0].astype(BF16), preferred_element_type=F32) + b_ref[0]


def _ada_call(cond, w_ada, b_ada):
    depth, d, n = w_ada.shape
    tn = _tile(n, 512)
    return pl.pallas_call(
        _ada_kernel,
        grid=(depth, n // tn),
        in_specs=[
            pl.BlockSpec((SUBLANES, d), lambda i, j: (0, 0)),
            pl.BlockSpec((1, d, tn), lambda i, j: (i, 0, j)),
            pl.BlockSpec((1, 1, tn), lambda i, j: (i, 0, j)),
        ],
        out_specs=pl.BlockSpec((1, SUBLANES, tn), lambda i, j: (i, 0, j)),
        out_shape=jax.ShapeDtypeStruct((depth, SUBLANES, n), F32),
        compiler_params=_params(("parallel", "parallel")),
        name="adaln",
    )(cond, w_ada, b_ada.reshape(depth, 1, n))


def _norm_kernel(*refs, modulate):
    x_ref, w_ref = refs[0], refs[1]
    o_ref = refs[-1]
    x = x_ref[0]
    y = x * lax.rsqrt(jnp.mean(x * x, axis=-1, keepdims=True) + RMS_EPS) * w_ref[...]
    if modulate:
        scale_ref, shift_ref = refs[2], refs[3]
        y = y * (1.0 + scale_ref[0]) + shift_ref[0]
    o_ref[0] = y.astype(o_ref.dtype)


def _norm_call(x, w, scale=None, shift=None, *, out_dtype):
    b, s, d = x.shape
    tm = _tile(s, 256)
    modulate = scale is not None
    row = pl.BlockSpec((1, tm, d), lambda i, j: (i, j, 0))
    in_specs = [row, pl.BlockSpec((1, d), lambda i, j: (0, 0))]
    args = [x, w.reshape(1, d)]
    if modulate:
        per_batch = pl.BlockSpec((1, 1, d), lambda i, j: (i, 0, 0))
        in_specs += [per_batch, per_batch]
        args += [scale.reshape(b, 1, d), shift.reshape(b, 1, d)]
    return pl.pallas_call(
        functools.partial(_norm_kernel, modulate=modulate),
        grid=(b, s // tm),
        in_specs=in_specs,
        out_specs=row,
        out_shape=jax.ShapeDtypeStruct((b, s, d), out_dtype),
        compiler_params=_params(("parallel", "parallel")),
        name="rmsnorm_mod" if modulate else "rmsnorm",
    )(*args)


def _norm_pair_kernel(x_ref, c_ref, w_ref, xscale_ref, xshift_ref, cscale_ref, cshift_ref, o_ref, *, n_x):
    def emit(src_ref, scale_ref, shift_ref):
        x = src_ref[0]
        y = x * lax.rsqrt(jnp.mean(x * x, axis=-1, keepdims=True) + RMS_EPS) * w_ref[...]
        o_ref[0] = (y * (1.0 + scale_ref[0]) + shift_ref[0]).astype(o_ref.dtype)

    from_x = pl.program_id(1) < n_x
    pl.when(from_x)(functools.partial(emit, x_ref, xscale_ref, xshift_ref))
    pl.when(jnp.logical_not(from_x))(functools.partial(emit, c_ref, cscale_ref, cshift_ref))


def _norm_pair_call(x, ctx, w, x_scale, x_shift, c_scale, c_shift, *, out_dtype):
    b, s, d = x.shape
    n_ctx_tok = ctx.shape[1]
    tm = _tile(n_ctx_tok, 256)
    assert s % tm == 0
    n_x, n_c = s // tm, n_ctx_tok // tm
    per_batch = pl.BlockSpec((1, 1, d), lambda i, j: (i, 0, 0))
    shared = pl.BlockSpec((1, 1, d), lambda i, j: (0, 0, 0))
    return pl.pallas_call(
        functools.partial(_norm_pair_kernel, n_x=n_x),
        grid=(b, n_x + n_c),
        in_specs=[pl.BlockSpec((1, tm, d), lambda i, j: (i, jnp.minimum(j, n_x - 1), 0)),
                  pl.BlockSpec((1, tm, d), lambda i, j: (i, jnp.maximum(j - n_x, 0), 0)),
                  pl.BlockSpec((1, d), lambda i, j: (0, 0)),
                  per_batch, per_batch, shared, shared],
        out_specs=pl.BlockSpec((1, tm, d), lambda i, j: (i, j, 0)),
        out_shape=jax.ShapeDtypeStruct((b, s + n_ctx_tok, d), out_dtype),
        compiler_params=_params(("parallel", "arbitrary")),
        name="rmsnorm_mod_pair",
    )(x, ctx, w.reshape(1, d), x_scale.reshape(b, 1, d), x_shift.reshape(b, 1, d),
      c_scale.reshape(1, 1, d), c_shift.reshape(1, 1, d))


def _mm_kernel(*refs, nk, n_extra, epilogue):
    a_ref, w_ref = refs[0], refs[1]
    extras = refs[2:2 + n_extra]
    o_ref = refs[2 + n_extra]
    a = a_ref[...]
    part = jnp.dot(a.reshape(-1, a.shape[-1]), w_ref[...], preferred_element_type=F32)

    def finish(acc):
        out = epilogue(acc, *[e[...] for e in extras])
        o_ref[...] = out.reshape(o_ref.shape).astype(o_ref.dtype)

    if nk == 1:
        finish(part)
        return
    acc_ref = refs[3 + n_extra]
    k = pl.program_id(2)

    @pl.when(k == 0)
    def _():
        acc_ref[...] = part

    @pl.when(jnp.logical_and(k > 0, k < nk - 1))
    def _():
        acc_ref[...] += part

    @pl.when(k == nk - 1)
    def _():
        finish(acc_ref[...] + part)


def _mm_call(a, w, extras, extra_specs, *, grid, a_spec, w_spec, out_spec, out_shape, out_dtype,
             epilogue, name):
    nk = grid[2]
    blk = [n for n in out_spec.block_shape if n is not None]
    scratch = [pltpu.VMEM((int(np.prod(blk[:-1])), blk[-1]), F32)] if nk > 1 else []
    return pl.pallas_call(
        functools.partial(_mm_kernel, nk=nk, n_extra=len(extras), epilogue=epilogue),
        grid=grid,
        in_specs=[a_spec, w_spec, *extra_specs],
        out_specs=out_spec,
        out_shape=jax.ShapeDtypeStruct(out_shape, out_dtype),
        scratch_shapes=scratch,
        compiler_params=_params(("parallel", "parallel", "arbitrary")),
        name=name,
    )(a, w, *extras)


def _in_proj(h, w):
    m, d = h.shape
    n = w.shape[1]
    tm, tn = _tile(m, 1024), _tile(n, 1024)
    return _mm_call(
        h, w, [], [],
        grid=(m // tm, n // tn, 1),
        a_spec=pl.BlockSpec((tm, d), lambda i, j, k: (i, 0)),
        w_spec=pl.BlockSpec((d, tn), lambda i, j, k: (0, j)),
        out_spec=pl.BlockSpec((tm, tn), lambda i, j, k: (i, j)),
        out_shape=(m, n), out_dtype=BF16, epilogue=lambda acc: acc, name="in_proj")


def _in_proj_tmajor(h, w):
    b, s, d = h.shape
    n = w.shape[1]
    c = s // S5_T
    tn = _tile(n, 1024)
    return _mm_call(
        h.reshape(b, c, S5_T * d), w, [], [],
        grid=(S5_T, n // tn, 1),
        a_spec=pl.BlockSpec((b, c, d), lambda t, j, k: (0, 0, t)),
        w_spec=pl.BlockSpec((d, tn), lambda t, j, k: (0, j)),
        out_spec=pl.BlockSpec((None, b, c, tn), lambda t, j, k: (t, 0, 0, j)),
        out_shape=(S5_T, b, c, n), out_dtype=BF16, epilogue=lambda acc: acc, name="in_proj_tmajor")


def _glu_epilogue(acc, bias, y, z):
    y = y.astype(F32)
    z = z.astype(F32).reshape(acc.shape)
    return y * jax.nn.sigmoid(acc + bias) * jax.nn.silu(z)


def _glu_gate(y, uz, w, bias, n_lat):
    m, e = y.shape
    t, b = uz.shape[0], uz.shape[1]
    tm = b * n_lat
    assert m == t * tm
    tn, tk = _tile(e, 1024), _tile(e, 2048)
    zoff = e // tn
    return _mm_call(
        y, w, [bias.reshape(1, e), y, uz],
        [pl.BlockSpec((1, tn), lambda i, j, k: (0, j)),
         pl.BlockSpec((tm, tn), lambda i, j, k: (i, j)),
         pl.BlockSpec((None, b, n_lat, tn), lambda i, j, k: (i, 0, 0, j + zoff))],
        grid=(t, e // tn, e // tk),
        a_spec=pl.BlockSpec((tm, tk), lambda i, j, k: (i, k)),
        w_spec=pl.BlockSpec((tk, tn), lambda i, j, k: (k, j)),
        out_spec=pl.BlockSpec((tm, tn), lambda i, j, k: (i, j)),
        out_shape=(m, e), out_dtype=BF16, epilogue=_glu_epilogue, name="s5_glu")


def _res_epilogue(acc, x, gate):
    return x + gate * acc.reshape(x.shape)


def _out_proj(v, w, x, gate):
    b, s, d = x.shape
    e = v.shape[1]
    tm, tn, tk = _tile(s, 1024), _tile(d, 1024), _tile(e, 2048)
    per_b = s // tm
    row = pl.BlockSpec((1, tm, tn), lambda i, j, k: (i // per_b, i % per_b, j))
    return _mm_call(
        v, w, [x, gate.reshape(b, 1, d)],
        [row, pl.BlockSpec((1, 1, tn), lambda i, j, k: (i // per_b, 0, j))],
        grid=(b * per_b, d // tn, e // tk),
        a_spec=pl.BlockSpec((tm, tk), lambda i, j, k: (i, k)),
        w_spec=pl.BlockSpec((tk, tn), lambda i, j, k: (k, j)),
        out_spec=row,
        out_shape=(b, s, d), out_dtype=F32, epilogue=_res_epilogue, name="out_proj")


def _out_proj_tmajor(v, w, x, gate):
    b, s, d = x.shape
    e = v.shape[1]
    c = s // S5_T
    tn, tk = _tile(d, 1024), _tile(e, 2048)
    per_t = d // tn
    row = pl.BlockSpec((b, c, tn), lambda t, j, k: (0, 0, t * per_t + j))
    out = _mm_call(
        v, w, [x.reshape(b, c, S5_T * d), gate.reshape(b, 1, d)],
        [row, pl.BlockSpec((b, 1, tn), lambda t, j, k: (0, 0, j))],
        grid=(S5_T, d // tn, e // tk),
        a_spec=pl.BlockSpec((b * c, tk), lambda t, j, k: (t, k)),
        w_spec=pl.BlockSpec((tk, tn), lambda t, j, k: (k, j)),
        out_spec=row,
        out_shape=(b, c, S5_T * d), out_dtype=F32, epilogue=_res_epilogue, name="out_proj_tmajor")
    return out.reshape(b, s, d)


def _pool_epilogue(acc, scale, z):
    return acc * scale * jax.nn.silu(z.astype(F32))


def _pool_proj(delta, uz, w, scale):
    m, e = delta.shape
    ng, dg, _ = w.shape
    tm, tn = _tile(m, 1024), _tile(dg, 1024)
    per_g = dg // tn
    zoff = e // tn
    return _mm_call(
        delta, w.reshape(ng * dg, dg), [scale.reshape(1, e), uz],
        [pl.BlockSpec((1, tn), lambda i, j, k: (0, j)),
         pl.BlockSpec((tm, tn), lambda i, j, k: (i, j + zoff))],
        grid=(m // tm, e // tn, 1),
        a_spec=pl.BlockSpec((tm, dg), lambda i, j, k: (i, j // per_g)),
        w_spec=pl.BlockSpec((dg, tn), lambda i, j, k: (j // per_g, j % per_g)),
        out_spec=pl.BlockSpec((tm, tn), lambda i, j, k: (i, j)),
        out_shape=(m, e), out_dtype=BF16, epilogue=_pool_epilogue, name="pool_proj")


def _s5_pitch(rows):
    pitch = -(-rows // SUBLANES) * SUBLANES
    if (pitch // SUBLANES) % 2 == 0:
        pitch += SUBLANES
    return pitch


def _swap_blocks(vs, width):
    n = len(vs)
    lane = lax.broadcasted_iota(jnp.int32, vs[0].shape, 1)
    total = n * width
    delta = n // 2
    while delta:
        upper = (lane & (delta * width)) != 0
        nxt = list(vs)
        for a in range(n):
            if a & delta:
                continue
            b = a + delta
            nxt[a] = jnp.where(upper, pltpu.roll(vs[b], delta * width, 1), vs[a])
            nxt[b] = jnp.where(upper, vs[b], pltpu.roll(vs[a], total - delta * width, 1))
        vs = nxt
        delta //= 2
    return vs


def _s5_kernel(x_ref, wm_ref, ws_ref, wo_ref, coef_ref, y_ref, u_ref, s_ref, *, n_ctx, n_lat):
    gb, rows_all, _ = u_ref.shape
    nb = x_ref.shape[1]
    n_seq = n_lat + n_ctx
    pitch = s_ref.shape[1] // gb
    jch = LANES // gb

    for half in range(S5_T // gb):
        xs = [pltpu.bitcast(x_ref[half * gb + t].reshape(rows_all, LANES), jnp.uint32) for t in range(gb)]
        for g, v in enumerate(_swap_blocks(xs, jch)):
            u_ref[g, :, half * LANES:(half + 1) * LANES] = pltpu.bitcast(v, BF16)

    for g in range(gb):
        s = jnp.dot(u_ref[g], ws_ref[g], preferred_element_type=F32)
        for q in range(4):
            s_ref[q, g * pitch:g * pitch + rows_all, :] = s[:, q * LANES:(q + 1) * LANES]

    ar_f, ai_f, ar_r, ai_r = coef_ref[0], coef_ref[1], coef_ref[2], coef_ref[3]

    def rows(slab, row):
        return s_ref.at[slab][pl.ds(row, gb, stride=pitch), :]

    def step(i, carry):
        rf = jnp.where(i < n_ctx, n_lat + i, i - n_ctx)
        rr = n_seq - 1 - i
        out = []
        for b in range(nb):
            hf, hfs, hr, hrs = carry[4 * b:4 * b + 4]
            row_f = b * n_seq + rf
            row_r = b * n_seq + rr
            sf, sfs = rows(0, row_f), rows(2, row_f)
            sr, srs = rows(1, row_r), rows(3, row_r)
            s_ref.at[0][pl.ds(row_f, gb, stride=pitch), :] = hf
            s_ref.at[1][pl.ds(row_r, gb, stride=pitch), :] = hr
            out += [ar_f * hf + ai_f * hfs + sf, ar_f * hfs - ai_f * hf + sfs,
                    ar_r * hr + ai_r * hrs + sr, ar_r * hrs - ai_r * hr + srs]
        return tuple(out)

    zero = jnp.zeros((gb, LANES), F32)
    lax.fori_loop(0, n_seq, step, (zero,) * (4 * nb))

    for g in range(gb):
        for b in range(nb):
            lo = b * n_seq
            h = jnp.concatenate([s_ref[0, g * pitch + lo:g * pitch + lo + n_lat, :],
                                 s_ref[1, g * pitch + lo:g * pitch + lo + n_lat, :]], axis=1)
            acc = jnp.dot(u_ref[g, lo:lo + n_lat, :], wm_ref[g], preferred_element_type=F32)
            acc += jnp.dot(h.astype(BF16), wo_ref[g], preferred_element_type=F32)
            u_ref[g, lo:lo + n_lat, :] = jax.nn.gelu(acc).astype(BF16)

    for half in range(S5_T // gb):
        ys = [pltpu.bitcast(u_ref[g, :, half * LANES:(half + 1) * LANES], jnp.uint32) for g in range(gb)]
        for t, v in enumerate(_swap_blocks(ys, jch)):
            v = pltpu.bitcast(v, BF16)
            for b in range(nb):
                y_ref[half * gb + t, b] = v[b * n_seq:b * n_seq + n_lat]


def _s5_call(ut, wm, ws, wo, coef, *, n_ctx, n_lat):
    t, nb, n_seq, _ = ut.shape
    g, width, _ = wm.shape
    gb = SUBLANES
    rows = nb * n_seq
    assert t == S5_T and n_seq == n_ctx + n_lat and g % gb == 0 and width == 2 * LANES
    assert rows % (2 * SUBLANES) == 0 and n_lat % (2 * SUBLANES) == 0
    pitch = _s5_pitch(rows)
    grp = lambda shape: pl.BlockSpec((gb,) + shape, lambda i: (i, 0, 0))
    return pl.pallas_call(
        functools.partial(_s5_kernel, n_ctx=n_ctx, n_lat=n_lat),
        grid=(g // gb,),
        in_specs=[pl.BlockSpec((t, nb, n_seq, LANES), lambda i: (0, 0, 0, i)),
                  grp((width, width)), grp((width, 4 * LANES)), grp((width, width)),
                  pl.BlockSpec((4, gb, LANES), lambda i: (0, i, 0))],
        out_specs=pl.BlockSpec((t, nb, n_lat, LANES), lambda i: (0, 0, 0, i)),
        out_shape=jax.ShapeDtypeStruct((t, nb, n_lat, g * width // t), BF16),
        scratch_shapes=[pltpu.VMEM((gb, rows, width), BF16),
                        pltpu.VMEM((4, gb * pitch, LANES), F32)],
        compiler_params=_params(("parallel",)),
        name="s5_scan",
    )(ut, wm, ws, wo, coef)


def _s5_weights(lam_re, lam_im, log_step, b_re, b_im, c_re, c_im, d_skip):
    _, g, p, j = b_re.shape
    t = S5_T
    gb = SUBLANES
    assert 2 * p == LANES and t * j == 2 * LANES and g % gb == 0
    twice = lambda a: jnp.concatenate([a, a], axis=-1).astype(F32)
    lam_re2, lam_im2 = twice(lam_re), twice(lam_im)
    ls2 = jnp.broadcast_to(log_step.astype(F32)[..., None], (2, g, 2 * p))
    b_re2, b_im2 = twice(b_re.transpose(0, 1, 3, 2)), twice(b_im.transpose(0, 1, 3, 2))
    c_re2, c_im2 = twice(c_re), twice(c_im)
    d_lane = jnp.tile(d_skip.astype(F32).reshape(g, 1, j), (1, 1, t))

    vec = pl.BlockSpec((2, gb, 2 * p), lambda i: (0, i, 0))
    mat = pl.BlockSpec((2, gb, j, 2 * p), lambda i: (0, i, 0, 0))
    grp = lambda shape: pl.BlockSpec((gb,) + shape, lambda i: (i, 0, 0))
    return pl.pallas_call(
        _s5_weights_kernel,
        grid=(g // gb,),
        in_specs=[vec, vec, vec, mat, mat, mat, mat, grp((1, t * j))],
        out_specs=[grp((t * j, t * j)), grp((t * j, 8 * p)), grp((4 * p, t * j)),
                   pl.BlockSpec((4, gb, 2 * p), lambda i: (0, i, 0))],
        out_shape=[jax.ShapeDtypeStruct((g, t * j, t * j), BF16),
                   jax.ShapeDtypeStruct((g, t * j, 8 * p), BF16),
                   jax.ShapeDtypeStruct((g, 4 * p, t * j), BF16),
                   jax.ShapeDtypeStruct((4, g, 2 * p), F32)],
        scratch_shapes=[pltpu.VMEM((2, (t + 1) * j, 2 * p), F32)],
        compiler_params=_params(("parallel",)),
        name="s5_weights",
    )(lam_re2, lam_im2, ls2, b_re2, b_im2, c_re2, c_im2, d_lane)


def _s5_weights_kernel(lre_ref, lim_ref, ls_ref, bre_ref, bim_ref, cre_ref, cim_ref, d_ref,
                       wm_ref, ws_ref, wo_ref, coef_ref, zt_ref):
    t = S5_T
    gb = wm_ref.shape[0]
    j = bre_ref.shape[2]
    width = t * j
    hi = lax.Precision.HIGHEST
    lo_half = lax.broadcasted_iota(jnp.int32, (1, LANES), 1) < LANES // 2
    pick = lambda first, second: jnp.where(lo_half, first, second)
    step_k = lax.broadcasted_iota(jnp.int32, (3 * SUBLANES, LANES), 0).astype(F32)
    lane_w = lax.broadcasted_iota(jnp.int32, (j, width), 1)
    row_w = lax.broadcasted_iota(jnp.int32, (j, width), 0)
    nt = (((1,), (1,)), ((), ()))

    def one_group(g, carry):
        kts = []
        for d in range(2):
            lr, li = lre_ref[d, pl.ds(g, 1), :], lim_ref[d, pl.ds(g, 1), :]
            dt = jnp.exp(ls_ref[d, pl.ds(g, 1), :])
            mag = jnp.exp(lr * dt * step_k)
            ang = li * dt * step_k
            p_re, p_im = mag * jnp.cos(ang), mag * jnp.sin(ang)
            power = lambda k: (p_re[k:k + 1], p_im[k:k + 1])

            e_re, e_im = power(1)
            den = lr * lr + li * li
            f_re = ((e_re - 1.0) * lr + e_im * li) / den
            f_im = (e_im * lr - (e_re - 1.0) * li) / den
            b_re, b_im = bre_ref[d, g], bim_ref[d, g]
            bb_re, bb_im = f_re * b_re - f_im * b_im, f_re * b_im + f_im * b_re
            bb_a, bb_b = pick(bb_re, bb_im), pick(-bb_im, bb_re)
            bb_c, bb_d = pick(bb_im, bb_re), pick(bb_re, -bb_im)
            c_re, c_im = cre_ref[d, g], cim_ref[d, g]
            cz_a, cz_b = pick(c_re, -c_im), pick(-c_im, -c_re)

            for s in range(t):
                pr, pi = power(t - 1 - s if d == 0 else s)
                rows = slice(s * j, (s + 1) * j)
                ws_ref[g, rows, d * LANES:(d + 1) * LANES] = (pr * bb_a + pi * bb_b).astype(BF16)
                ws_ref[g, rows, (2 + d) * LANES:(3 + d) * LANES] = (pr * bb_c + pi * bb_d).astype(BF16)

            for m in range(t + 1):
                pr, pi = power(m if d == 0 else t - m)
                zt_ref[d, m * j:(m + 1) * j, :] = pr * cz_a + pi * cz_b

            z_kern = zt_ref[d, 0:width, :] if d == 0 else zt_ref[d, j:j + width, :]
            kts.append(lax.dot_general(bb_a, z_kern, nt, precision=hi, preferred_element_type=F32))

            pr, pi = power(t)
            coef_ref[2 * d, pl.ds(g, 1), :] = pr
            coef_ref[2 * d + 1, pl.ds(g, 1), :] = pick(-pi, pi)

        wo_t = jnp.concatenate([zt_ref[0, j:j + width, :], zt_ref[1, 0:width, :]], axis=1)
        wo_ref[g] = wo_t.T.astype(BF16)

        d_lane = d_ref[g]
        for s in range(t):
            fwd = jnp.where(lane_w >= s * j, pltpu.roll(kts[0], s * j, 1), 0.0)
            rev = jnp.where(lane_w < (s + 1) * j, pltpu.roll(kts[1], (width - (t - 1 - s) * j) % width, 1), 0.0)
            skip = jnp.where(lane_w - s * j == row_w, d_lane, 0.0)
            wm_ref[g, s * j:(s + 1) * j, :] = (fwd + rev + skip).astype(BF16)
        return carry

    lax.fori_loop(0, gb, one_group, 0)


def _pool_kernel(u_ref, pc_ref, invc_ref, o_ref, cs_ref, *, n_rows, per_group):
    slab = pc_ref.shape[1]
    group = pl.program_id(1) // per_group

    def run(w):
        for s in range(n_rows * GRID_W // slab):
            sl = slice(s * slab, (s + 1) * slab)
            cs_ref[sl, :] = jnp.dot(pc_ref[0], u_ref[0, sl, :], preferred_element_type=F32)
        lo, hi = w // 2, w - w // 2
        inv_c = invc_ref[0]

        def grid_row(r):
            return pl.ds(pl.multiple_of(r * GRID_W, GRID_W), GRID_W)

        def emit(r, taps):
            acc = cs_ref[grid_row(r + taps[0]), :]
            for dr in taps[1:]:
                acc += cs_ref[grid_row(r + dr), :]
            own = grid_row(r)
            mean = acc * (inv_c * (1.0 / len(taps)))
            o_ref[0, own, :] = (mean - u_ref[0, own, :].astype(F32)).astype(o_ref.dtype)

        full = list(range(-lo, hi))
        first_full, last_full = lo, n_rows - hi
        for r in list(range(first_full)) + list(range(last_full + 1, n_rows)):
            emit(r, [dr for dr in full if 0 <= r + dr < n_rows])

        def body(r, carry):
            emit(r, full)
            return carry

        lax.fori_loop(first_full, last_full + 1, body, 0)

    for k, w in enumerate(POOL_WINDOWS):
        pl.when(group == k)(functools.partial(run, w))


def _pool_consts(slab):
    pcs, invs = [], []
    pos = np.arange(GRID_W)
    for w in POOL_WINDOWS:
        lo, hi = np.clip(pos - w // 2, 0, GRID_W), np.clip(pos + w - w // 2, 0, GRID_W)
        inside = (pos[None, :] >= lo[:, None]) & (pos[None, :] < hi[:, None])
        pcs.append(np.kron(np.eye(slab // GRID_W), inside.astype(np.float32)))
        invs.append((1.0 / (hi - lo)).astype(np.float32).reshape(GRID_W, 1))
    return jnp.asarray(np.stack(pcs), BF16), jnp.asarray(np.stack(invs), F32)


def _pool_call(uz, e):
    b, l, _ = uz.shape
    n_rows = l // GRID_W
    dg = e // len(POOL_WINDOWS)
    tc = _tile(dg, 256)
    slab = _tile(l, 256)
    pc, invc = _pool_consts(slab)
    per_group = dg // tc
    blk = pl.BlockSpec((1, l, tc), lambda i, j: (i, 0, j))
    return pl.pallas_call(
        functools.partial(_pool_kernel, n_rows=n_rows, per_group=per_group),
        grid=(b, e // tc),
        in_specs=[blk,
                  pl.BlockSpec((1, slab, slab), lambda i, j: (j // per_group, 0, 0)),
                  pl.BlockSpec((1, GRID_W, 1), lambda i, j: (j // per_group, 0, 0))],
        out_specs=blk,
        out_shape=jax.ShapeDtypeStruct((b, l, e), BF16),
        scratch_shapes=[pltpu.VMEM((l, tc), F32)],
        compiler_params=_params(("parallel", "parallel")),
        name="pool_delta",
    )(uz, pc, invc)


def kernel(x, c, ctx, c_ctx, norm_w, w_ada, b_ada, w_in, w_out, s5_lam_re, s5_lam_im, s5_log_step,
           s5_b_re, s5_b_im, s5_c_re, s5_c_im, s5_d, s5_w_glu, s5_b_glu, pool_w, pool_scale,
           final_norm_w):
    bsz, seq, d = x.shape
    n_ctx_tok = ctx.shape[1]
    e = w_in.shape[2] // 2
    jch = s5_b_re.shape[4]
    assert bsz + 1 <= SUBLANES and S5_T * jch == 2 * LANES and 2 * s5_b_re.shape[3] == LANES
    assert seq % S5_T == 0 and n_ctx_tok % S5_T == 0 and seq % GRID_W == 0

    cond = jnp.zeros((SUBLANES, d), F32).at[:bsz].set(c).at[bsz].set(c_ctx)
    mods = _ada_call(cond, w_ada, b_ada)
    shift, scale, gate = mods[..., :d], mods[..., d:2 * d], mods[..., 2 * d:]

    h = _norm_pair_call(x, ctx, norm_w[0], scale[0, :bsz], shift[0, :bsz], scale[0, bsz], shift[0, bsz],
                        out_dtype=BF16)
    uz = _in_proj_tmajor(h, w_in[0].astype(BF16))
    n_ctx, n_lat = n_ctx_tok // S5_T, seq // S5_T
    wm, ws, wo, coef = _s5_weights(s5_lam_re[0], s5_lam_im[0], s5_log_step[0], s5_b_re[0], s5_b_im[0],
                                   s5_c_re[0], s5_c_im[0], s5_d[0])
    y = _s5_call(uz, wm, ws, wo, coef, n_ctx=n_ctx, n_lat=n_lat)
    v = _glu_gate(y.reshape(S5_T * bsz * n_lat, e), uz, s5_w_glu[0].astype(BF16), s5_b_glu[0], n_lat)
    x1 = _out_proj_tmajor(v, w_out[0].astype(BF16), x, gate[0, :bsz])

    h = _norm_call(x1, norm_w[1], scale[1, :bsz], shift[1, :bsz], out_dtype=BF16)
    uz = _in_proj(h.reshape(bsz * seq, d), w_in[1].astype(BF16))
    delta = _pool_call(uz.reshape(bsz, seq, 2 * e), e)
    v = _pool_proj(delta.reshape(bsz * seq, e), uz, pool_w[0].astype(BF16), pool_scale[0])
    x2 = _out_proj(v, w_out[1].astype(BF16), x1, gate[1, :bsz])

    return _norm_call(x2, final_norm_w, out_dtype=x.dtype)
```

```python
import functools

import jax
import jax.numpy as jnp
import numpy as np
from jax import lax
from jax.experimental import pallas as pl
from jax.experimental.pallas import tpu as pltpu

F32 = jnp.float32
BF16 = jnp.bfloat16

GRID_W = 64
POOL_WINDOWS = (2, 4, 8, 16)
RMS_EPS = 1e-6
S5_T = 16
LANES = 128
SUBLANES = 8
VMEM_LIMIT = 56 * 1024 * 1024


def _params(sem):
    return pltpu.CompilerParams(dimension_semantics=sem, vmem_limit_bytes=VMEM_LIMIT)


def _tile(n, pref):
    t = min(n, pref)
    while n % t:
        t //= 2
    return t


def _ada_kernel(c_ref, w_ref, b_ref, o_ref):
    s = jax.nn.silu(c_ref[...]).astype(BF16)
    o_ref[0] = jnp.dot(s, w_ref[0].astype(BF16), preferred_element_type=F32) + b_ref[0]


def _ada_call(cond, w_ada, b_ada):
    depth, d, n = w_ada.shape
    tn = _tile(n, 512)
    return pl.pallas_call(
        _ada_kernel,
        grid=(depth, n // tn),
        in_specs=[
            pl.BlockSpec((SUBLANES, d), lambda i, j: (0, 0)),
            pl.BlockSpec((1, d, tn), lambda i, j: (i, 0, j)),
            pl.BlockSpec((1, 1, tn), lambda i, j: (i, 0, j)),
        ],
        out_specs=pl.BlockSpec((1, SUBLANES, tn), lambda i, j: (i, 0, j)),
        out_shape=jax.ShapeDtypeStruct((depth, SUBLANES, n), F32),
        compiler_params=_params(("parallel", "parallel")),
        name="adaln",
    )(cond, w_ada, b_ada.reshape(depth, 1, n))


def _norm_kernel(*refs, modulate):
    x_ref, w_ref = refs[0], refs[1]
    o_ref = refs[-1]
    x = x_ref[0]
    y = x * lax.rsqrt(jnp.mean(x * x, axis=-1, keepdims=True) + RMS_EPS) * w_ref[...]
    if modulate:
        scale_ref, shift_ref = refs[2], refs[3]
        y = y * (1.0 + scale_ref[0]) + shift_ref[0]
    o_ref[0] = y.astype(o_ref.dtype)


def _norm_call(x, w, scale=None, shift=None, *, out_dtype):
    b, s, d = x.shape
    tm = _tile(s, 256)
    modulate = scale is not None
    row = pl.BlockSpec((1, tm, d), lambda i, j: (i, j, 0))
    in_specs = [row, pl.BlockSpec((1, d), lambda i, j: (0, 0))]
    args = [x, w.reshape(1, d)]
    if modulate:
        per_batch = pl.BlockSpec((1, 1, d), lambda i, j: (i, 0, 0))
        in_specs += [per_batch, per_batch]
        args += [scale.reshape(b, 1, d), shift.reshape(b, 1, d)]
    return pl.pallas_call(
        functools.partial(_norm_kernel, modulate=modulate),
        grid=(b, s // tm),
        in_specs=in_specs,
        out_specs=row,
        out_shape=jax.ShapeDtypeStruct((b, s, d), out_dtype),
        compiler_params=_params(("parallel", "parallel")),
        name="rmsnorm_mod" if modulate else "rmsnorm",
    )(*args)


def _norm_pair_kernel(x_ref, c_ref, w_ref, xscale_ref, xshift_ref, cscale_ref, cshift_ref, o_ref, *, n_x):
    def emit(src_ref, scale_ref, shift_ref):
        x = src_ref[0]
        y = x * lax.rsqrt(jnp.mean(x * x, axis=-1, keepdims=True) + RMS_EPS) * w_ref[...]
        o_ref[0] = (y * (1.0 + scale_ref[0]) + shift_ref[0]).astype(o_ref.dtype)

    from_x = pl.program_id(1) < n_x
    pl.when(from_x)(functools.partial(emit, x_ref, xscale_ref, xshift_ref))
    pl.when(jnp.logical_not(from_x))(functools.partial(emit, c_ref, cscale_ref, cshift_ref))


def _norm_pair_call(x, ctx, w, x_scale, x_shift, c_scale, c_shift, *, out_dtype):
    b, s, d = x.shape
    n_ctx_tok = ctx.shape[1]
    tm = _tile(n_ctx_tok, 256)
    assert s % tm == 0
    n_x, n_c = s // tm, n_ctx_tok // tm
    per_batch = pl.BlockSpec((1, 1, d), lambda i, j: (i, 0, 0))
    shared = pl.BlockSpec((1, 1, d), lambda i, j: (0, 0, 0))
    return pl.pallas_call(
        functools.partial(_norm_pair_kernel, n_x=n_x),
        grid=(b, n_x + n_c),
        in_specs=[pl.BlockSpec((1, tm, d), lambda i, j: (i, jnp.minimum(j, n_x - 1), 0)),
                  pl.BlockSpec((1, tm, d), lambda i, j: (i, jnp.maximum(j - n_x, 0), 0)),
                  pl.BlockSpec((1, d), lambda i, j: (0, 0)),
                  per_batch, per_batch, shared, shared],
        out_specs=pl.BlockSpec((1, tm, d), lambda i, j: (i, j, 0)),
        out_shape=jax.ShapeDtypeStruct((b, s + n_ctx_tok, d), out_dtype),
        compiler_params=_params(("parallel", "arbitrary")),
        name="rmsnorm_mod_pair",
    )(x, ctx, w.reshape(1, d), x_scale.reshape(b, 1, d), x_shift.reshape(b, 1, d),
      c_scale.reshape(1, 1, d), c_shift.reshape(1, 1, d))


def _mm_kernel(*refs, nk, n_extra, epilogue):
    a_ref, w_ref = refs[0], refs[1]
    extras = refs[2:2 + n_extra]
    o_ref = refs[2 + n_extra]
    a = a_ref[...]
    part = jnp.dot(a.reshape(-1, a.shape[-1]), w_ref[...], preferred_element_type=F32)

    def finish(acc):
        out = epilogue(acc, *[e[...] for e in extras])
        o_ref[...] = out.reshape(o_ref.shape).astype(o_ref.dtype)

    if nk == 1:
        finish(part)
        return
    acc_ref = refs[3 + n_extra]
    k = pl.program_id(2)

    @pl.when(k == 0)
    def _():
        acc_ref[...] = part

    @pl.when(jnp.logical_and(k > 0, k < nk - 1))
    def _():
        acc_ref[...] += part

    @pl.when(k == nk - 1)
    def _():
        finish(acc_ref[...] + part)


def _mm_call(a, w, extras, extra_specs, *, grid, a_spec, w_spec, out_spec, out_shape, out_dtype,
             epilogue, name):
    nk = grid[2]
    blk = [n for n in out_spec.block_shape if n is not None]
    scratch = [pltpu.VMEM((int(np.prod(blk[:-1])), blk[-1]), F32)] if nk > 1 else []
    return pl.pallas_call(
        functools.partial(_mm_kernel, nk=nk, n_extra=len(extras), epilogue=epilogue),
        grid=grid,
        in_specs=[a_spec, w_spec, *extra_specs],
        out_specs=out_spec,
        out_shape=jax.ShapeDtypeStruct(out_shape, out_dtype),
        scratch_shapes=scratch,
        compiler_params=_params(("parallel", "parallel", "arbitrary")),
        name=name,
    )(a, w, *extras)


def _in_proj(h, w):
    m, d = h.shape
    n = w.shape[1]
    tm, tn = _tile(m, 1024), _tile(n, 1024)
    return _mm_call(
        h, w, [], [],
        grid=(m // tm, n // tn, 1),
        a_spec=pl.BlockSpec((tm, d), lambda i, j, k: (i, 0)),
        w_spec=pl.BlockSpec((d, tn), lambda i, j, k: (0, j)),
        out_spec=pl.BlockSpec((tm, tn), lambda i, j, k: (i, j)),
        out_shape=(m, n), out_dtype=BF16, epilogue=lambda acc: acc, name="in_proj")


def _in_proj_tmajor(h, w):
    b, s, d = h.shape
    n = w.shape[1]
    c = s // S5_T
    tn = _tile(n, 1024)
    return _mm_call(
        h.reshape(b, c, S5_T * d), w, [], [],
        grid=(S5_T, n // tn, 1),
        a_spec=pl.BlockSpec((b, c, d), lambda t, j, k: (0, 0, t)),
        w_spec=pl.BlockSpec((d, tn), lambda t, j, k: (0, j)),
        out_spec=pl.BlockSpec((None, b, c, tn), lambda t, j, k: (t, 0, 0, j)),
        out_shape=(S5_T, b, c, n), out_dtype=BF16, epilogue=lambda acc: acc, name="in_proj_tmajor")


def _glu_epilogue(acc, bias, y, z):
    y = y.astype(F32)
    z = z.astype(F32).reshape(acc.shape)
    return y * jax.nn.sigmoid(acc + bias) * jax.nn.silu(z)


def _glu_gate(y, uz, w, bias, n_lat):
    m, e = y.shape
    t, b = uz.shape[0], uz.shape[1]
    tm = b * n_lat
    assert m == t * tm
    tn, tk = _tile(e, 1024), _tile(e, 2048)
    zoff = e // tn
    return _mm_call(
        y, w, [bias.reshape(1, e), y, uz],
        [pl.BlockSpec((1, tn), lambda i, j, k: (0, j)),
         pl.BlockSpec((tm, tn), lambda i, j, k: (i, j)),
         pl.BlockSpec((None, b, n_lat, tn), lambda i, j, k: (i, 0, 0, j + zoff))],
        grid=(t, e // tn, e // tk),
        a_spec=pl.BlockSpec((tm, tk), lambda i, j, k: (i, k)),
        w_spec=pl.BlockSpec((tk, tn), lambda i, j, k: (k, j)),
        out_spec=pl.BlockSpec((tm, tn), lambda i, j, k: (i, j)),
        out_shape=(m, e), out_dtype=BF16, epilogue=_glu_epilogue, name="s5_glu")


def _res_epilogue(acc, x, gate):
    return x + gate * acc.reshape(x.shape)


def _out_proj(v, w, x, gate):
    b, s, d = x.shape
    e = v.shape[1]
    tm, tn, tk = _tile(s, 1024), _tile(d, 1024), _tile(e, 2048)
    per_b = s // tm
    row = pl.BlockSpec((1, tm, tn), lambda i, j, k: (i // per_b, i % per_b, j))
    return _mm_call(
        v, w, [x, gate.reshape(b, 1, d)],
        [row, pl.BlockSpec((1, 1, tn), lambda i, j, k: (i // per_b, 0, j))],
        grid=(b * per_b, d // tn, e // tk),
        a_spec=pl.BlockSpec((tm, tk), lambda i, j, k: (i, k)),
        w_spec=pl.BlockSpec((tk, tn), lambda i, j, k: (k, j)),
        out_spec=row,
        out_shape=(b, s, d), out_dtype=F32, epilogue=_res_epilogue, name="out_proj")


def _out_proj_tmajor(v, w, x, gate):
    b, s, d = x.shape
    e = v.shape[1]
    c = s // S5_T
    tn, tk = _tile(d, 1024), _tile(e, 2048)
    per_t = d // tn
    row = pl.BlockSpec((b, c, tn), lambda t, j, k: (0, 0, t * per_t + j))
    out = _mm_call(
        v, w, [x.reshape(b, c, S5_T * d), gate.reshape(b, 1, d)],
        [row, pl.BlockSpec((b, 1, tn), lambda t, j, k: (0, 0, j))],
        grid=(S5_T, d // tn, e // tk),
        a_spec=pl.BlockSpec((b * c, tk), lambda t, j, k: (t, k)),
        w_spec=pl.BlockSpec((tk, tn), lambda t, j, k: (k, j)),
        out_spec=row,
        out_shape=(b, c, S5_T * d), out_dtype=F32, epilogue=_res_epilogue, name="out_proj_tmajor")
    return out.reshape(b, s, d)


def _pool_epilogue(acc, scale, z):
    return acc * scale * jax.nn.silu(z.astype(F32))


def _pool_proj(delta, uz, w, scale):
    m, e = delta.shape
    ng, dg, _ = w.shape
    tm, tn = _tile(m, 1024), _tile(dg, 1024)
    per_g = dg // tn
    zoff = e // tn
    return _mm_call(
        delta, w.reshape(ng * dg, dg), [scale.reshape(1, e), uz],
        [pl.BlockSpec((1, tn), lambda i, j, k: (0, j)),
         pl.BlockSpec((tm, tn), lambda i, j, k: (i, j + zoff))],
        grid=(m // tm, e // tn, 1),
        a_spec=pl.BlockSpec((tm, dg), lambda i, j, k: (i, j // per_g)),
        w_spec=pl.BlockSpec((dg, tn), lambda i, j, k: (j // per_g, j % per_g)),
        out_spec=pl.BlockSpec((tm, tn), lambda i, j, k: (i, j)),
        out_shape=(m, e), out_dtype=BF16, epilogue=_pool_epilogue, name="pool_proj")


def _s5_pitch(rows):
    pitch = -(-rows // SUBLANES) * SUBLANES
    if (pitch // SUBLANES) % 2 == 0:
        pitch += SUBLANES
    return pitch


def _swap_blocks(vs, width):
    n = len(vs)
    lane = lax.broadcasted_iota(jnp.int32, vs[0].shape, 1)
    total = n * width
    delta = n // 2
    while delta:
        upper = (lane & (delta * width)) != 0
        nxt = list(vs)
        for a in range(n):
            if a & delta:
                continue
            b = a + delta
            nxt[a] = jnp.where(upper, pltpu.roll(vs[b], delta * width, 1), vs[a])
            nxt[b] = jnp.where(upper, vs[b], pltpu.roll(vs[a], total - delta * width, 1))
        vs = nxt
        delta //= 2
    return vs


def _s5_kernel(x_ref, wm_ref, ws_ref, wo_ref, coef_ref, y_ref, u_ref, s_ref, *, n_ctx, n_lat):
    gb, rows_all, _ = u_ref.shape
    nb = x_ref.shape[1]
    n_seq = n_lat + n_ctx
    pitch = s_ref.shape[1] // gb
    jch = LANES // gb

    rt = 2 * SUBLANES

    def regroup_in(ct, carry):
        c0 = pl.multiple_of(ct * rt, rt)
        for b in range(nb):
            for half in range(S5_T // gb):
                xs = [pltpu.bitcast(x_ref[half * gb + t, b, pl.ds(c0, rt), :], jnp.uint32) for t in range(gb)]
                for g, v in enumerate(_swap_blocks(xs, jch)):
                    u_ref[g, pl.ds(b * n_seq + c0, rt), half * LANES:(half + 1) * LANES] = pltpu.bitcast(v, BF16)
        return carry

    lax.fori_loop(0, n_seq // rt, regroup_in, 0)

    for g in range(gb):
        s = jnp.dot(u_ref[g], ws_ref[g], preferred_element_type=F32)
        for q in range(4):
            s_ref[q, g * pitch:g * pitch + rows_all, :] = s[:, q * LANES:(q + 1) * LANES]

    ar_f, ai_f, ar_r, ai_r = coef_ref[0], coef_ref[1], coef_ref[2], coef_ref[3]

    def rows(slab, row):
        return s_ref.at[slab][pl.ds(row, gb, stride=pitch), :]

    def step(i, carry):
        rf = jnp.where(i < n_ctx, n_lat + i, i - n_ctx)
        rr = n_seq - 1 - i
        out = []
        for b in range(nb):
            hf, hfs, hr, hrs = carry[4 * b:4 * b + 4]
            row_f = b * n_seq + rf
            row_r = b * n_seq + rr
            sf, sfs = rows(0, row_f), rows(2, row_f)
            sr, srs = rows(1, row_r), rows(3, row_r)
            s_ref.at[0][pl.ds(row_f, gb, stride=pitch), :] = hf
            s_ref.at[1][pl.ds(row_r, gb, stride=pitch), :] = hr
            out += [ar_f * hf + ai_f * hfs + sf, ar_f * hfs - ai_f * hf + sfs,
                    ar_r * hr + ai_r * hrs + sr, ar_r * hrs - ai_r * hr + srs]
        return tuple(out)

    zero = jnp.zeros((gb, LANES), F32)
    lax.fori_loop(0, n_seq, step, (zero,) * (4 * nb))

    for g in range(gb):
        for b in range(nb):
            lo = b * n_seq
            h = jnp.concatenate([s_ref[0, g * pitch + lo:g * pitch + lo + n_lat, :],
                                 s_ref[1, g * pitch + lo:g * pitch + lo + n_lat, :]], axis=1)
            acc = jnp.dot(u_ref[g, lo:lo + n_lat, :], wm_ref[g], preferred_element_type=F32)
            acc += jnp.dot(h.astype(BF16), wo_ref[g], preferred_element_type=F32)
            u_ref[g, lo:lo + n_lat, :] = jax.nn.gelu(acc).astype(BF16)

    def regroup_out(ct, carry):
        c0 = pl.multiple_of(ct * rt, rt)
        for b in range(nb):
            for half in range(S5_T // gb):
                ys = [pltpu.bitcast(u_ref[g, pl.ds(b * n_seq + c0, rt), half * LANES:(half + 1) * LANES], jnp.uint32)
                      for g in range(gb)]
                for t, v in enumerate(_swap_blocks(ys, jch)):
                    y_ref[half * gb + t, b, pl.ds(c0, rt), :] = pltpu.bitcast(v, BF16)
        return carry

    lax.fori_loop(0, n_lat // rt, regroup_out, 0)


def _s5_call(ut, wm, ws, wo, coef, *, n_ctx, n_lat):
    t, nb, n_seq, _ = ut.shape
    g, width, _ = wm.shape
    gb = SUBLANES
    rows = nb * n_seq
    assert t == S5_T and n_seq == n_ctx + n_lat and g % gb == 0 and width == 2 * LANES
    assert n_seq % (2 * SUBLANES) == 0 and n_lat % (2 * SUBLANES) == 0
    pitch = _s5_pitch(rows)
    grp = lambda shape: pl.BlockSpec((gb,) + shape, lambda i: (i, 0, 0))
    return pl.pallas_call(
        functools.partial(_s5_kernel, n_ctx=n_ctx, n_lat=n_lat),
        grid=(g // gb,),
        in_specs=[pl.BlockSpec((t, nb, n_seq, LANES), lambda i: (0, 0, 0, i)),
                  grp((width, width)), grp((width, 4 * LANES)), grp((width, width)),
                  pl.BlockSpec((4, gb, LANES), lambda i: (0, i, 0))],
        out_specs=pl.BlockSpec((t, nb, n_lat, LANES), lambda i: (0, 0, 0, i)),
        out_shape=jax.ShapeDtypeStruct((t, nb, n_lat, g * width // t), BF16),
        scratch_shapes=[pltpu.VMEM((gb, rows, width), BF16),
                        pltpu.VMEM((4, gb * pitch, LANES), F32)],
        compiler_params=_params(("parallel",)),
        name="s5_scan",
    )(ut, wm, ws, wo, coef)


def _s5_weights(lam_re, lam_im, log_step, b_re, b_im, c_re, c_im, d_skip):
    _, g, p, j = b_re.shape
    t = S5_T
    gb = SUBLANES
    assert 2 * p == LANES and t * j == 2 * LANES and g % gb == 0
    twice = lambda a: jnp.concatenate([a, a], axis=-1).astype(F32)
    lam_re2, lam_im2 = twice(lam_re), twice(lam_im)
    ls2 = jnp.broadcast_to(log_step.astype(F32)[..., None], (2, g, 2 * p))
    b_re2, b_im2 = twice(b_re.transpose(0, 1, 3, 2)), twice(b_im.transpose(0, 1, 3, 2))
    c_re2, c_im2 = twice(c_re), twice(c_im)
    d_lane = jnp.tile(d_skip.astype(F32).reshape(g, 1, j), (1, 1, t))

    vec = pl.BlockSpec((2, gb, 2 * p), lambda i: (0, i, 0))
    mat = pl.BlockSpec((2, gb, j, 2 * p), lambda i: (0, i, 0, 0))
    grp = lambda shape: pl.BlockSpec((gb,) + shape, lambda i: (i, 0, 0))
    return pl.pallas_call(
        _s5_weights_kernel,
        grid=(g // gb,),
        in_specs=[vec, vec, vec, mat, mat, mat, mat, grp((1, t * j))],
        out_specs=[grp((t * j, t * j)), grp((t * j, 8 * p)), grp((4 * p, t * j)),
                   pl.BlockSpec((4, gb, 2 * p), lambda i: (0, i, 0))],
        out_shape=[jax.ShapeDtypeStruct((g, t * j, t * j), BF16),
                   jax.ShapeDtypeStruct((g, t * j, 8 * p), BF16),
                   jax.ShapeDtypeStruct((g, 4 * p, t * j), BF16),
                   jax.ShapeDtypeStruct((4, g, 2 * p), F32)],
        scratch_shapes=[pltpu.VMEM((2, (t + 1) * j, 2 * p), F32)],
        compiler_params=_params(("parallel",)),
        name="s5_weights",
    )(lam_re2, lam_im2, ls2, b_re2, b_im2, c_re2, c_im2, d_lane)


def _s5_weights_kernel(lre_ref, lim_ref, ls_ref, bre_ref, bim_ref, cre_ref, cim_ref, d_ref,
                       wm_ref, ws_ref, wo_ref, coef_ref, zt_ref):
    t = S5_T
    gb = wm_ref.shape[0]
    j = bre_ref.shape[2]
    width = t * j
    hi = lax.Precision.HIGHEST
    lo_half = lax.broadcasted_iota(jnp.int32, (1, LANES), 1) < LANES // 2
    pick = lambda first, second: jnp.where(lo_half, first, second)
    step_k = lax.broadcasted_iota(jnp.int32, (3 * SUBLANES, LANES), 0).astype(F32)
    lane_w = lax.broadcasted_iota(jnp.int32, (j, width), 1)
    row_w = lax.broadcasted_iota(jnp.int32, (j, width), 0)
    nt = (((1,), (1,)), ((), ()))

    def one_group(g, carry):
        kts = []
        for d in range(2):
            lr, li = lre_ref[d, pl.ds(g, 1), :], lim_ref[d, pl.ds(g, 1), :]
            dt = jnp.exp(ls_ref[d, pl.ds(g, 1), :])
            mag = jnp.exp(lr * dt * step_k)
            ang = li * dt * step_k
            p_re, p_im = mag * jnp.cos(ang), mag * jnp.sin(ang)
            power = lambda k: (p_re[k:k + 1], p_im[k:k + 1])

            e_re, e_im = power(1)
            den = lr * lr + li * li
            f_re = ((e_re - 1.0) * lr + e_im * li) / den
            f_im = (e_im * lr - (e_re - 1.0) * li) / den
            b_re, b_im = bre_ref[d, g], bim_ref[d, g]
            bb_re, bb_im = f_re * b_re - f_im * b_im, f_re * b_im + f_im * b_re
            bb_a, bb_b = pick(bb_re, bb_im), pick(-bb_im, bb_re)
            bb_c, bb_d = pick(bb_im, bb_re), pick(bb_re, -bb_im)
            c_re, c_im = cre_ref[d, g], cim_ref[d, g]
            cz_a, cz_b = pick(c_re, -c_im), pick(-c_im, -c_re)

            for s in range(t):
                pr, pi = power(t - 1 - s if d == 0 else s)
                rows = slice(s * j, (s + 1) * j)
                ws_ref[g, rows, d * LANES:(d + 1) * LANES] = (pr * bb_a + pi * bb_b).astype(BF16)
                ws_ref[g, rows, (2 + d) * LANES:(3 + d) * LANES] = (pr * bb_c + pi * bb_d).astype(BF16)

            for m in range(t + 1):
                pr, pi = power(m if d == 0 else t - m)
                zt_ref[d, m * j:(m + 1) * j, :] = pr * cz_a + pi * cz_b

            z_kern = zt_ref[d, 0:width, :] if d == 0 else zt_ref[d, j:j + width, :]
            kts.append(lax.dot_general(bb_a, z_kern, nt, precision=hi, preferred_element_type=F32))

            pr, pi = power(t)
            coef_ref[2 * d, pl.ds(g, 1), :] = pr
            coef_ref[2 * d + 1, pl.ds(g, 1), :] = pick(-pi, pi)

        wo_t = jnp.concatenate([zt_ref[0, j:j + width, :], zt_ref[1, 0:width, :]], axis=1)
        wo_ref[g] = wo_t.T.astype(BF16)

        d_lane = d_ref[g]
        for s in range(t):
            fwd = jnp.where(lane_w >= s * j, pltpu.roll(kts[0], s * j, 1), 0.0)
            rev = jnp.where(lane_w < (s + 1) * j, pltpu.roll(kts[1], (width - (t - 1 - s) * j) % width, 1), 0.0)
            skip = jnp.where(lane_w - s * j == row_w, d_lane, 0.0)
            wm_ref[g, s * j:(s + 1) * j, :] = (fwd + rev + skip).astype(BF16)
        return carry

    lax.fori_loop(0, gb, one_group, 0)


def _pool_kernel(u_ref, pc_ref, invc_ref, o_ref, cs_ref, *, n_rows, per_group):
    slab = pc_ref.shape[1]
    group = pl.program_id(1) // per_group

    def run(w):
        for s in range(n_rows * GRID_W // slab):
            sl = slice(s * slab, (s + 1) * slab)
            cs_ref[sl, :] = jnp.dot(pc_ref[0], u_ref[0, sl, :], preferred_element_type=F32)
        lo, hi = w // 2, w - w // 2
        inv_c = invc_ref[0]

        def grid_row(r):
            return pl.ds(pl.multiple_of(r * GRID_W, GRID_W), GRID_W)

        def emit(r, taps):
            acc = cs_ref[grid_row(r + taps[0]), :]
            for dr in taps[1:]:
                acc += cs_ref[grid_row(r + dr), :]
            own = grid_row(r)
            mean = acc * (inv_c * (1.0 / len(taps)))
            o_ref[0, own, :] = (mean - u_ref[0, own, :].astype(F32)).astype(o_ref.dtype)

        full = list(range(-lo, hi))
        first_full, last_full = lo, n_rows - hi
        for r in list(range(first_full)) + list(range(last_full + 1, n_rows)):
            emit(r, [dr for dr in full if 0 <= r + dr < n_rows])

        def body(r, carry):
            emit(r, full)
            return carry

        lax.fori_loop(first_full, last_full + 1, body, 0)

    for k, w in enumerate(POOL_WINDOWS):
        pl.when(group == k)(functools.partial(run, w))


def _pool_consts(slab):
    pcs, invs = [], []
    pos = np.arange(GRID_W)
    for w in POOL_WINDOWS:
        lo, hi = np.clip(pos - w // 2, 0, GRID_W), np.clip(pos + w - w // 2, 0, GRID_W)
        inside = (pos[None, :] >= lo[:, None]) & (pos[None, :] < hi[:, None])
        pcs.append(np.kron(np.eye(slab // GRID_W), inside.astype(np.float32)))
        invs.append((1.0 / (hi - lo)).astype(np.float32).reshape(GRID_W, 1))
    return jnp.asarray(np.stack(pcs), BF16), jnp.asarray(np.stack(invs), F32)


def _pool_call(uz, e):
    b, l, _ = uz.shape
    n_rows = l // GRID_W
    dg = e // len(POOL_WINDOWS)
    tc = _tile(dg, 256)
    slab = _tile(l, 256)
    pc, invc = _pool_consts(slab)
    per_group = dg // tc
    blk = pl.BlockSpec((1, l, tc), lambda i, j: (i, 0, j))
    return pl.pallas_call(
        functools.partial(_pool_kernel, n_rows=n_rows, per_group=per_group),
        grid=(b, e // tc),
        in_specs=[blk,
                  pl.BlockSpec((1, slab, slab), lambda i, j: (j // per_group, 0, 0)),
                  pl.BlockSpec((1, GRID_W, 1), lambda i, j: (j // per_group, 0, 0))],
        out_specs=blk,
        out_shape=jax.ShapeDtypeStruct((b, l, e), BF16),
        scratch_shapes=[pltpu.VMEM((l, tc), F32)],
        compiler_params=_params(("parallel", "parallel")),
        name="pool_delta",
    )(uz, pc, invc)


def kernel(x, c, ctx, c_ctx, norm_w, w_ada, b_ada, w_in, w_out, s5_lam_re, s5_lam_im, s5_log_step,
           s5_b_re, s5_b_im, s5_c_re, s5_c_im, s5_d, s5_w_glu, s5_b_glu, pool_w, pool_scale,
           final_norm_w):
    bsz, seq, d = x.shape
    n_ctx_tok = ctx.shape[1]
    e = w_in.shape[2] // 2
    jch = s5_b_re.shape[4]
    assert bsz + 1 <= SUBLANES and S5_T * jch == 2 * LANES and 2 * s5_b_re.shape[3] == LANES
    assert seq % S5_T == 0 and n_ctx_tok % S5_T == 0 and seq % GRID_W == 0

    cond = jnp.zeros((SUBLANES, d), F32).at[:bsz].set(c).at[bsz].set(c_ctx)
    mods = _ada_call(cond, w_ada, b_ada)
    shift, scale, gate = mods[..., :d], mods[..., d:2 * d], mods[..., 2 * d:]

    h = _norm_pair_call(x, ctx, norm_w[0], scale[0, :bsz], shift[0, :bsz], scale[0, bsz], shift[0, bsz],
                        out_dtype=BF16)
    uz = _in_proj_tmajor(h, w_in[0].astype(BF16))
    n_ctx, n_lat = n_ctx_tok // S5_T, seq // S5_T
    wm, ws, wo, coef = _s5_weights(s5_lam_re[0], s5_lam_im[0], s5_log_step[0], s5_b_re[0], s5_b_im[0],
                                   s5_c_re[0], s5_c_im[0], s5_d[0])
    y = _s5_call(uz, wm, ws, wo, coef, n_ctx=n_ctx, n_lat=n_lat)
    v = _glu_gate(y.reshape(S5_T * bsz * n_lat, e), uz, s5_w_glu[0].astype(BF16), s5_b_glu[0], n_lat)
    x1 = _out_proj_tmajor(v, w_out[0].astype(BF16), x, gate[0, :bsz])

    h = _norm_call(x1, norm_w[1], scale[1, :bsz], shift[1, :bsz], out_dtype=BF16)
    uz = _in_proj(h.reshape(bsz * seq, d), w_in[1].astype(BF16))
    delta = _pool_call(uz.reshape(bsz, seq, 2 * e), e)
    v = _pool_proj(delta.reshape(bsz * seq, e), uz, pool_w[0].astype(BF16), pool_scale[0])
    x2 = _out_proj(v, w_out[1].astype(BF16), x1, gate[1, :bsz])

    return _norm_call(x2, final_norm_w, out_dtype=x.dtype)
```
